```python
import math
import jax, jax.numpy as jnp
from jax import lax
import numpy as np

D_MODEL = 2048
BATCH = 4
SEQ = 2048
DEPTH = 2
DEC_BATCH = 128
DEC_SEQ = 4
PAST_LEN = 8192
PAGE_SIZE = 128

N_A = DEPTH // 2
N_B = DEPTH - N_A
RET_HEADS = D_MODEL // 256
RET_DK = D_MODEL // RET_HEADS
RET_DV = 2 * RET_DK
RET_CHUNK = 128
ROPE_THETA = 10000.0
MLA_HEADS = D_MODEL // 128
NOPE_DIM = 128
ROPE_DIM = 64
V_DIM = 128
QK_DIM = NOPE_DIM + ROPE_DIM
KV_LORA = D_MODEL // 4
Q_BLOCK = 128
N_GROUPS = 4
EXP_PER_GROUP = 4
N_EXPERTS = N_GROUPS * EXP_PER_GROUP
TOP_K = 2
EXP_FF = D_MODEL // 2
EPS = 1e-6

kernel_name = 'yoco_retention_mla_hmoe_step'


def rmsnorm(x, g=None):
    xf = x.astype(jnp.float32)
    y = xf * lax.rsqrt(jnp.mean(xf * xf, axis=-1, keepdims=True) + EPS)
    if g is not None:
        y = y * g.astype(jnp.float32)
    return y.astype(x.dtype)


def rope(x, pos):
    d = x.shape[-1]
    freq = ROPE_THETA ** (-jnp.arange(0, d, 2, dtype=jnp.float32) / d)
    ang = pos.astype(jnp.float32)[:, None] * freq[None, :]
    cos = jnp.cos(ang)[:, None, :]
    sin = jnp.sin(ang)[:, None, :]
    xf = x.astype(jnp.float32)
    x1, x2 = xf[..., : d // 2], xf[..., d // 2:]
    return jnp.concatenate([x1 * cos - x2 * sin, x1 * sin + x2 * cos], axis=-1).astype(x.dtype)


def rope_tail(x, pos):
    return jnp.concatenate([x[..., :NOPE_DIM], rope(x[..., NOPE_DIM:], pos)], axis=-1)


def retention_chunk(S, q, k, v, log_g):
    L = q.shape[1]
    idx = jnp.arange(L, dtype=jnp.float32)
    rel = idx[:, None] - idx[None, :]
    decay = jnp.where(rel >= 0, jnp.exp(log_g[:, None, None] * jnp.maximum(rel, 0.0)), 0.0)
    S = S.astype(jnp.float32)
    qf, kf, vf = q.astype(jnp.float32), k.astype(jnp.float32), v.astype(jnp.float32)
    scores = jnp.einsum('bihd,bjhd->bhij', qf, kf) * decay
    inner = jnp.einsum('bhij,bjhe->bihe', scores, vf)
    cross = jnp.einsum('bihd,bhde->bihe', qf, S) * jnp.exp((idx[:, None] + 1.0) * log_g[None, :])[None, :, :, None]
    k_dec = kf * jnp.exp((L - 1.0 - idx)[:, None] * log_g[None, :])[None, :, :, None]
    S_new = jnp.exp(L * log_g)[None, :, None, None] * S + jnp.einsum('bjhd,bjhe->bhde', k_dec, vf)
    return S_new, inner + cross


def retention_mixer(h, S0, pos, w_in, w_out, log_g):
    B, T, _ = h.shape
    HK = RET_HEADS * RET_DK
    HV = RET_HEADS * RET_DV
    q, k, v, g = jnp.split(h @ w_in, [HK, 2 * HK, 2 * HK + HV], axis=-1)
    q = rope(q.reshape(B, T, RET_HEADS, RET_DK), pos)
    k = rope(k.reshape(B, T, RET_HEADS, RET_DK), pos) * (RET_DK ** -0.5)
    v = v.reshape(B, T, RET_HEADS, RET_DV)
    C = RET_CHUNK if T % RET_CHUNK == 0 else T
    n = T // C

    def to_chunks(a):
        return a.reshape(B, n, C, *a.shape[2:]).swapaxes(0, 1)

    def step(S, qkv):
        return retention_chunk(S, qkv[0], qkv[1], qkv[2], log_g)

    S_fin, o = lax.scan(step, S0.astype(jnp.float32), (to_chunks(q), to_chunks(k), to_chunks(v)))
    o = o.swapaxes(0, 1).reshape(B, T, RET_HEADS, RET_DV)
    o = rmsnorm(o).reshape(B, T, HV).astype(h.dtype)
    return (jax.nn.silu(g) * o) @ w_out, S_fin


def shared_latent(x, g_src, w_dkv, g_ckv):
    s = rmsnorm(x, g_src)
    ck = s @ w_dkv
    return rmsnorm(ck[..., :KV_LORA], g_ckv), ck[..., KV_LORA:]


def make_keys(ckv, kpe, pos, w_uk, g_k):
    k_nope = jnp.einsum('...lc,chn->...lhn', ckv, w_uk)
    k_pe = jnp.broadcast_to(kpe[..., None, :], k_nope.shape[:-1] + (ROPE_DIM,))
    k = rmsnorm(jnp.concatenate([k_nope, k_pe], axis=-1), g_k)
    return rope_tail(k, pos)


def make_queries(h, pos, w_q, g_q):
    q = (h @ w_q).reshape(*h.shape[:-1], MLA_HEADS, QK_DIM)
    return rope_tail(rmsnorm(q, g_q), pos)


def attend(q, k, ckv, q_pos, k_pos, w_uv):
    s = jnp.einsum('...qhd,...khd->...hqk', q, k).astype(jnp.float32) * (QK_DIM ** -0.5)
    s = jnp.where(k_pos[None, :] <= q_pos[:, None], s, -jnp.inf)
    p = jax.nn.softmax(s, axis=-1).astype(ckv.dtype)
    ctx = jnp.einsum('...hqk,...kc->...qhc', p, ckv)
    return jnp.einsum('...qhc,chv->...qhv', ctx, w_uv)


def mla_prompt(h, ckv, k, pos, w_q, g_q, w_uv, w_o):
    B, T, _ = h.shape
    q = make_queries(h, pos, w_q, g_q)
    nb = T // Q_BLOCK
    qb = q.reshape(B, nb, Q_BLOCK, MLA_HEADS, QK_DIM).swapaxes(0, 1)
    pb = pos.reshape(nb, Q_BLOCK)
    out = lax.map(lambda a: attend(a[0], k, ckv, a[1], pos, w_uv), (qb, pb))
    return out.swapaxes(0, 1).reshape(B, T, MLA_HEADS * V_DIM) @ w_o


def mla_sample(h, ckv_new, kpe_new, cache_ckv, cache_kpe, page_table, w_q, g_q, w_uk, w_uv, g_k, w_o):
    Bd, T, _ = h.shape
    past = page_table.shape[1] * cache_ckv.shape[1]
    q_pos = past + jnp.arange(T)
    k_pos = jnp.arange(past + T)
    q = make_queries(h, q_pos, w_q, g_q)

    def one_seq(a):
        q_b, pt_b, cn_b, kn_b = a
        ckv_b = jnp.concatenate([cache_ckv[pt_b].reshape(-1, KV_LORA), cn_b.astype(cache_ckv.dtype)], axis=0)
        kpe_b = jnp.concatenate([cache_kpe[pt_b].reshape(-1, ROPE_DIM), kn_b.astype(cache_kpe.dtype)], axis=0)
        k_b = make_keys(ckv_b, kpe_b, k_pos, w_uk, g_k)
        return attend(q_b, k_b, ckv_b, q_pos, k_pos, w_uv)

    out = lax.map(one_seq, (q, page_table, ckv_new, kpe_new))
    return out.reshape(Bd, T, MLA_HEADS * V_DIM).astype(h.dtype) @ w_o


def hier_moe(h, w_gr, b_gr, w_er, b_er, w1, w3, w2):
    shp = h.shape
    t = h.reshape(-1, shp[-1])
    gl = (t @ w_gr + b_gr).astype(jnp.float32)
    grp = jnp.argmax(gl, axis=-1)
    pg = jnp.max(jax.nn.softmax(gl, axis=-1), axis=-1)
    el = (t @ w_er + b_er).astype(jnp.float32).reshape(-1, N_GROUPS, EXP_PER_GROUP)
    el_sel = jnp.take_along_axis(el, grp[:, None, None], axis=1)[:, 0]
    top_v, top_i = lax.top_k(el_sel, TOP_K)
    w_grp = jnp.einsum('tk,tke->te', jax.nn.softmax(top_v, axis=-1),
                       jax.nn.one_hot(top_i, EXP_PER_GROUP, dtype=jnp.float32))
    comb = (pg[:, None, None] * jax.nn.one_hot(grp, N_GROUPS, dtype=jnp.float32)[:, :, None]
            * w_grp[:, None, :]).reshape(-1, N_EXPERTS).astype(h.dtype)
    a = jnp.einsum('td,edf->tef', t, w1)
    b = jnp.einsum('td,edf->tef', t, w3)
    y = jnp.einsum('tef,efd->td', jax.nn.silu(a) * b * comb[:, :, None], w2)
    return y.reshape(shp)


def setup_inputs(seed: int = 0) -> dict:
    key = jax.random.key(seed)
    ks = jax.random.split(key, 32)
    n_pages = PAST_LEN // PAGE_SIZE
    n_used = DEC_BATCH * n_pages
    n_pool = n_used + max(1, n_used // 4)
    f32 = jnp.float32

    def w(k, shape, fan_in):
        return jax.random.normal(k, shape, f32) * (fan_in ** -0.5)

    def gain(k, shape):
        return 1.0 + 0.02 * jax.random.normal(k, shape, f32)

    HK = RET_HEADS * RET_DK
    HV = RET_HEADS * RET_DV
    perm = jax.random.permutation(ks[0], n_pool)
    page_table = perm[:n_used].reshape(DEC_BATCH, n_pages).astype(jnp.int32)
    return {
        'x_prompt': jax.random.normal(ks[1], (BATCH, SEQ, D_MODEL), f32),
        'x_sample': jax.random.normal(ks[2], (DEC_BATCH, DEC_SEQ, D_MODEL), f32),
        'state_ret': 0.5 * jax.random.normal(ks[3], (N_A, DEC_BATCH, RET_HEADS, RET_DK, RET_DV), f32),
        'cache_ckv': jax.random.normal(ks[4], (n_pool, PAGE_SIZE, KV_LORA), f32),
        'cache_kpe': jax.random.normal(ks[5], (n_pool, PAGE_SIZE, ROPE_DIM), f32),
        'page_table': page_table,
        'ln_mix': gain(ks[6], (DEPTH, D_MODEL)),
        'ln_ffn': gain(ks[7], (DEPTH, D_MODEL)),
        'ret_w_in': w(ks[8], (N_A, D_MODEL, 2 * HK + 2 * HV), D_MODEL),
        'ret_w_out': w(ks[9], (N_A, HV, D_MODEL), HV),
        'kv_src_norm': gain(ks[10], (D_MODEL,)),
        'w_dkv': w(ks[11], (D_MODEL, KV_LORA + ROPE_DIM), D_MODEL),
        'ckv_norm': gain(ks[12], (KV_LORA,)),
        'w_ukv': w(ks[13], (KV_LORA, MLA_HEADS, NOPE_DIM + V_DIM), KV_LORA),
        'k_norm': gain(ks[14], (QK_DIM,)),
        'mla_w_q': w(ks[15], (N_B, D_MODEL, MLA_HEADS * QK_DIM), D_MODEL),
        'q_norm': gain(ks[16], (N_B, QK_DIM)),
        'mla_w_o': w(ks[17], (N_B, MLA_HEADS * V_DIM, D_MODEL), MLA_HEADS * V_DIM),
        'moe_w_gr': w(ks[18], (DEPTH, D_MODEL, N_GROUPS), D_MODEL),
        'moe_b_gr': 0.01 * jax.random.normal(ks[19], (DEPTH, N_GROUPS), f32),
        'moe_w_er': w(ks[20], (DEPTH, D_MODEL, N_EXPERTS), D_MODEL),
        'moe_b_er': 0.01 * jax.random.normal(ks[21], (DEPTH, N_EXPERTS), f32),
        'moe_w1': w(ks[22], (DEPTH, N_EXPERTS, D_MODEL, EXP_FF), D_MODEL),
        'moe_w3': w(ks[23], (DEPTH, N_EXPERTS, D_MODEL, EXP_FF), D_MODEL),
        'moe_w2': w(ks[24], (DEPTH, N_EXPERTS, EXP_FF, D_MODEL), EXP_FF),
    }


def reference(x_prompt, x_sample, state_ret, cache_ckv, cache_kpe, page_table,
              ln_mix, ln_ffn, ret_w_in, ret_w_out, kv_src_norm, w_dkv, ckv_norm, w_ukv, k_norm,
              mla_w_q, q_norm, mla_w_o, moe_w_gr, moe_b_gr, moe_w_er, moe_b_er, moe_w1, moe_w3, moe_w2):
    Bp, Tp, _ = x_prompt.shape
    Bs, Ts, _ = x_sample.shape
    past = page_table.shape[1] * cache_ckv.shape[1]
    pos_p = jnp.arange(Tp)
    pos_s = past + jnp.arange(Ts)
    log_g = jnp.log1p(-jnp.exp2(-5.0 - jnp.arange(RET_HEADS, dtype=jnp.float32)))
    w_uk = w_ukv[..., :NOPE_DIM]
    w_uv = w_ukv[..., NOPE_DIM:]

    yp, ys = x_prompt, x_sample
    ret_p, ret_s = [], []
    for i in range(DEPTH):
        if i == N_A:
            ckv_p, kpe_p = shared_latent(yp, kv_src_norm, w_dkv, ckv_norm)
            ckv_s, kpe_s = shared_latent(ys, kv_src_norm, w_dkv, ckv_norm)
            k_p = make_keys(ckv_p, kpe_p, pos_p, w_uk, k_norm)
        hp = rmsnorm(yp, ln_mix[i])
        hs = rmsnorm(ys, ln_mix[i])
        if i < N_A:
            S0 = jnp.zeros((Bp, RET_HEADS, RET_DK, RET_DV), jnp.float32)
            mp, Sp = retention_mixer(hp, S0, pos_p, ret_w_in[i], ret_w_out[i], log_g)
            ms, Ss = retention_mixer(hs, state_ret[i], pos_s, ret_w_in[i], ret_w_out[i], log_g)
            ret_p.append(Sp.astype(x_prompt.dtype))
            ret_s.append(Ss.astype(state_ret.dtype))
        else:
            j = i - N_A
            mp = mla_prompt(hp, ckv_p, k_p, pos_p, mla_w_q[j], q_norm[j], w_uv, mla_w_o[j])
            ms = mla_sample(hs, ckv_s, kpe_s, cache_ckv, cache_kpe, page_table,
                            mla_w_q[j], q_norm[j], w_uk, w_uv, k_norm, mla_w_o[j])
        yp = yp + mp
        ys = ys + ms
        yp = yp + hier_moe(rmsnorm(yp, ln_ffn[i]), moe_w_gr[i], moe_b_gr[i], moe_w_er[i], moe_b_er[i],
                           moe_w1[i], moe_w3[i], moe_w2[i])
        ys = ys + hier_moe(rmsnorm(ys, ln_ffn[i]), moe_w_gr[i], moe_b_gr[i], moe_w_er[i], moe_b_er[i],
                           moe_w1[i], moe_w3[i], moe_w2[i])

    state_ret_prompt = jnp.stack(ret_p, axis=0)
    state_ret_sample = jnp.stack(ret_s, axis=0)
    return (yp, ys, state_ret_prompt, state_ret_sample, ckv_p, kpe_p, ckv_s, kpe_s)
```

```python
import functools
import math

import jax
import jax.numpy as jnp
from jax import lax
from jax.experimental import pallas as pl
from jax.experimental.pallas import tpu as pltpu

F32 = jnp.float32
BF16 = jnp.bfloat16
EPS = 1e-6
ROPE_THETA = 10000.0
RET_CHUNK = 128
NOPE_DIM = 128
ROPE_DIM = 64
V_DIM = 128
QK_DIM = NOPE_DIM + ROPE_DIM
QK_PAD = 256
N_GROUPS = 4
EXP_PER_GROUP = 4
PAGES_PER_STEP = 8
SUB_POS = 256
MIB = 1024 * 1024


def _params(sem, vmem_mib=48):
    return pltpu.CompilerParams(dimension_semantics=sem, vmem_limit_bytes=vmem_mib * MIB)


def _tile(n, pref):
    t = math.gcd(n, pref)
    assert t % 8 == 0 or t == n, (n, pref)
    return t


def _dot(a, b):
    return jnp.dot(a, b, preferred_element_type=F32)


def _dot_nt(a, b):
    return lax.dot_general(a, b, (((1,), (1,)), ((), ())), preferred_element_type=F32)


def _rms_scale(x, n):
    return lax.rsqrt(jnp.sum(x * x, axis=-1, keepdims=True) * (1.0 / n) + EPS)


def _rope_half(x, cos, sin):
    half = x.shape[-1] // 2
    x1, x2 = x[:, :half], x[:, half:]
    return jnp.concatenate([x1 * cos - x2 * sin, x1 * sin + x2 * cos], axis=-1)


def _mm_kernel(*refs, has_norm, has_res):
    it = iter(refs)
    x_ref = next(it)
    g_ref = next(it) if has_norm else None
    w_ref = next(it)
    r_ref = next(it) if has_res else None
    o_ref = next(it)
    xb_ref = next(it)

    @pl.when(pl.program_id(1) == 0)
    def _():
        x = x_ref[...].astype(F32)
        if has_norm:
            x = x * _rms_scale(x, x.shape[-1]) * g_ref[...]
        xb_ref[...] = x.astype(BF16)

    acc = _dot(xb_ref[...], w_ref[...])
    if has_res:
        acc = r_ref[...] + acc
    o_ref[...] = acc.astype(o_ref.dtype)


def _matmul(x, w, *, g=None, res=None, rows=None, tm, tn, out_dtype=F32, name):
    M, K = x.shape
    N = w.shape[1]
    rows = M if rows is None else rows
    in_specs = [pl.BlockSpec((tm, K), lambda i, j: (i, 0))]
    args = [x]
    if g is not None:
        in_specs.append(pl.BlockSpec((1, K), lambda i, j: (0, 0)))
        args.append(g.reshape(1, K))
    in_specs.append(pl.BlockSpec((K, tn), lambda i, j: (0, j)))
    args.append(w)
    if res is not None:
        in_specs.append(pl.BlockSpec((tm, tn), lambda i, j: (i, j)))
        args.append(res)
    return pl.pallas_call(
        functools.partial(_mm_kernel, has_norm=g is not None, has_res=res is not None),
        grid=(rows // tm, N // tn),
        in_specs=in_specs,
        out_specs=pl.BlockSpec((tm, tn), lambda i, j: (i, j)),
        out_shape=jax.ShapeDtypeStruct((rows, N), out_dtype),
        scratch_shapes=[pltpu.VMEM((tm, K), BF16)],
        compiler_params=_params(("parallel", "arbitrary")),
        name=name,
    )(*args)


def _ret_prompt_kernel(lg_ref, q_ref, k_ref, v_ref, g_ref, cos_ref, sin_ref, o_ref, s_out_ref, s_ref):
    h = pl.program_id(1)
    c = pl.program_id(2)
    lg = lg_ref[h]
    L, dk = q_ref.shape

    @pl.when(c == 0)
    def _():
        s_ref[...] = jnp.zeros_like(s_ref)

    cos, sin = cos_ref[...], sin_ref[...]
    q = _rope_half(q_ref[...], cos, sin)
    k = _rope_half(k_ref[...], cos, sin) * (dk ** -0.5)
    vb = v_ref[...].astype(BF16)
    qb = q.astype(BF16)
    ii = lax.broadcasted_iota(jnp.int32, (L, L), 0)
    jj = lax.broadcasted_iota(jnp.int32, (L, L), 1)
    rel = (ii - jj).astype(F32)
    decay = jnp.where(rel >= 0, jnp.exp(lg * jnp.maximum(rel, 0.0)), 0.0)
    scores = _dot_nt(qb, k.astype(BF16)) * decay
    inner = _dot(scores.astype(BF16), vb)
    s_old = s_ref[...]
    ri = lax.broadcasted_iota(jnp.int32, (L, 1), 0).astype(F32)
    cross = _dot(qb, s_old.astype(BF16)) * jnp.exp((ri + 1.0) * lg)
    k_dec = k * jnp.exp((L - 1.0 - ri) * lg)
    g_all = jnp.exp(jnp.zeros((1, 1), F32) + L * lg)
    s_new = g_all * s_old + _dot(k_dec.T.astype(BF16), vb)
    s_ref[...] = s_new
    o = inner + cross
    o = o * _rms_scale(o, o.shape[-1])
    gate = g_ref[...]
    o_ref[...] = (gate * jax.nn.sigmoid(gate)) * o

    @pl.when(c == pl.num_programs(2) - 1)
    def _():
        s_out_ref[0, 0] = s_new


def _retention_prompt(qkvg, log_g, cos, sin, *, B, T, H, dk, dv, M):
    C = RET_CHUNK if T % RET_CHUNK == 0 else T
    nC = T // C
    assert (2 * H * dk) % dv == 0
    voff = 2 * H * dk // dv
    return pl.pallas_call(
        _ret_prompt_kernel,
        grid=(B, H, nC),
        in_specs=[
            pl.BlockSpec(memory_space=pltpu.SMEM),
            pl.BlockSpec((C, dk), lambda b, h, c: (b * nC + c, h)),
            pl.BlockSpec((C, dk), lambda b, h, c: (b * nC + c, H + h)),
            pl.BlockSpec((C, dv), lambda b, h, c: (b * nC + c, voff + h)),
            pl.BlockSpec((C, dv), lambda b, h, c: (b * nC + c, voff + H + h)),
            pl.BlockSpec((C, dk // 2), lambda b, h, c: (c, 0)),
            pl.BlockSpec((C, dk // 2), lambda b, h, c: (c, 0)),
        ],
        out_specs=[
            pl.BlockSpec((C, dv), lambda b, h, c: (b * nC + c, h)),
            pl.BlockSpec((1, 1, dk, dv), lambda b, h, c: (b, h, 0, 0)),
        ],
        out_shape=[
            jax.ShapeDtypeStruct((M, H * dv), F32),
            jax.ShapeDtypeStruct((B, H, dk, dv), F32),
        ],
        scratch_shapes=[pltpu.VMEM((dk, dv), F32)],
        compiler_params=_params(("parallel", "parallel", "arbitrary")),
        name="retention_prompt",
    )(log_g, qkvg, qkvg, qkvg, qkvg, cos, sin)


def _ret_sample_kernel(lg_ref, q_ref, k_ref, v_ref, g_ref, cos_ref, sin_ref, s_in_ref, o_alias_ref,
                       o_ref, s_out_ref, *, T):
    del o_alias_ref
    h = pl.program_id(1)
    lg = lg_ref[h]
    R, dk = q_ref.shape
    dv = v_ref.shape[1]
    cos, sin = cos_ref[...], sin_ref[...]
    q = _rope_half(q_ref[...], cos, sin)
    k = _rope_half(k_ref[...], cos, sin) * (dk ** -0.5)
    v = v_ref[...]
    qb = q.astype(BF16)
    ii = lax.broadcasted_iota(jnp.int32, (R, R), 0)
    jj = lax.broadcasted_iota(jnp.int32, (R, R), 1)
    rel = (ii - jj).astype(F32)
    same = (ii // T) == (jj // T)
    decay = jnp.where(same, jnp.where(rel >= 0, jnp.exp(lg * jnp.maximum(rel, 0.0)), 0.0), 0.0)
    scores = _dot_nt(qb, k.astype(BF16)) * decay
    sc = scores.astype(BF16).astype(F32)
    vr = v.astype(BF16).astype(F32)
    inner = jnp.zeros((R, dv), F32)
    for j in range(R):
        inner = inner + sc[:, j:j + 1] * vr[j:j + 1, :]
    row = lax.broadcasted_iota(jnp.int32, (R, 1), 0)
    tok = (row % T).astype(F32)
    k_dec = k * jnp.exp((T - 1.0 - tok) * lg)
    g_all = jnp.exp(jnp.zeros((1, 1), F32) + T * lg)
    pad = RET_CHUNK - R
    v_pad = jnp.concatenate([v, jnp.zeros((pad, dv), F32)], axis=0).astype(BF16)
    cross = jnp.zeros((R, dv), F32)
    for s in range(R // T):
        mine = (row // T) == s
        s_old = s_in_ref[s, 0]
        cross = jnp.where(mine, _dot(qb, s_old.astype(BF16)), cross)
        kd = jnp.where(mine, k_dec, 0.0)
        kd_t = jnp.concatenate([kd, jnp.zeros((pad, dk), F32)], axis=0).T.astype(BF16)
        s_out_ref[s, 0] = g_all * s_old + _dot(kd_t, v_pad)
    o = inner + cross * jnp.exp((tok + 1.0) * lg)
    o = o * _rms_scale(o, dv)
    gate = g_ref[...]
    o_ref[...] = (gate * jax.nn.sigmoid(gate)) * o


def _retention_sample(qkvg, log_g, cos, sin, state, gated, *, Bd, T, H, dk, dv, Mp):
    R = 8
    assert R % T == 0 and Bd % (R // T) == 0 and Mp % R == 0
    spb = R // T
    rb0 = Mp // R
    voff = 2 * H * dk // dv
    M = qkvg.shape[0]
    return pl.pallas_call(
        functools.partial(_ret_sample_kernel, T=T),
        grid=(Bd // spb, H),
        in_specs=[
            pl.BlockSpec(memory_space=pltpu.SMEM),
            pl.BlockSpec((R, dk), lambda i, h: (rb0 + i, h)),
            pl.BlockSpec((R, dk), lambda i, h: (rb0 + i, H + h)),
            pl.BlockSpec((R, dv), lambda i, h: (rb0 + i, voff + h)),
            pl.BlockSpec((R, dv), lambda i, h: (rb0 + i, voff + H + h)),
            pl.BlockSpec((R, dk // 2), lambda i, h: (0, 0)),
            pl.BlockSpec((R, dk // 2), lambda i, h: (0, 0)),
            pl.BlockSpec((spb, 1, dk, dv), lambda i, h: (i, h, 0, 0)),
            pl.BlockSpec(memory_space=pl.ANY),
        ],
        out_specs=[
            pl.BlockSpec((R, dv), lambda i, h: (rb0 + i, h)),
            pl.BlockSpec((spb, 1, dk, dv), lambda i, h: (i, h, 0, 0)),
        ],
        out_shape=[
            jax.ShapeDtypeStruct((M, H * dv), F32),
            jax.ShapeDtypeStruct((Bd, H, dk, dv), F32),
        ],
        input_output_aliases={8: 0},
        compiler_params=_params(("parallel", "parallel")),
        name="retention_sample",
    )(log_g, qkvg, qkvg, qkvg, qkvg, cos, sin, state, gated)


def _router_kernel(y_ref, g_ref, wr_ref, br_ref, hb_ref, route_ref, cnt_ref, carry_ref):
    i = pl.program_id(0)
    tm = y_ref.shape[0]
    NE = N_GROUPS * EXP_PER_GROUP

    @pl.when(i == 0)
    def _():
        carry_ref[...] = jnp.zeros_like(carry_ref)

    y = y_ref[...]
    hn = y * _rms_scale(y, y.shape[-1]) * g_ref[...]
    hb_ref[...] = hn.astype(BF16)
    lt = lax.dot_general(wr_ref[...], hn, (((1,), (1,)), ((), ())),
                         precision=lax.Precision.HIGHEST, preferred_element_type=F32) + br_ref[...]
    gl = [lt[k:k + 1, :] for k in range(N_GROUPS)]
    gmax = functools.reduce(jnp.maximum, gl)
    grp = jnp.full(gmax.shape, N_GROUPS - 1, jnp.int32)
    for k in range(N_GROUPS - 2, -1, -1):
        grp = jnp.where(gl[k] == gmax, k, grp)
    pg = 1.0 / functools.reduce(lambda a, b: a + b, [jnp.exp(x - gmax) for x in gl])
    el = []
    for k in range(EXP_PER_GROUP):
        sel = lt[N_GROUPS + (N_GROUPS - 1) * EXP_PER_GROUP + k:N_GROUPS + (N_GROUPS - 1) * EXP_PER_GROUP + k + 1, :]
        for gi in range(N_GROUPS - 2, -1, -1):
            r = N_GROUPS + gi * EXP_PER_GROUP + k
            sel = jnp.where(grp == gi, lt[r:r + 1, :], sel)
        el.append(sel)
    v1 = functools.reduce(jnp.maximum, el)
    i1 = jnp.full(v1.shape, EXP_PER_GROUP - 1, jnp.int32)
    for k in range(EXP_PER_GROUP - 2, -1, -1):
        i1 = jnp.where(el[k] == v1, k, i1)
    el2 = [jnp.where(i1 == k, -jnp.inf, el[k]) for k in range(EXP_PER_GROUP)]
    v2 = functools.reduce(jnp.maximum, el2)
    i2 = jnp.full(v2.shape, EXP_PER_GROUP - 1, jnp.int32)
    for k in range(EXP_PER_GROUP - 2, -1, -1):
        i2 = jnp.where(el2[k] == v2, k, i2)
    e21 = jnp.exp(v2 - v1)
    den = 1.0 / (1.0 + e21)
    c1 = pg * den
    c2 = pg * (e21 * den)
    e1 = grp * EXP_PER_GROUP + i1
    e2 = grp * EXP_PER_GROUP + i2
    eid = lax.broadcasted_iota(jnp.int32, (NE, tm), 0)
    oh1 = (eid == e1).astype(F32)
    oh2 = (eid == e2).astype(F32)
    oh = oh1 + oh2
    tr = lax.broadcasted_iota(jnp.int32, (tm, tm), 0)
    tc = lax.broadcasted_iota(jnp.int32, (tm, tm), 1)
    before = jnp.where(tr < tc, 1.0, 0.0).astype(BF16)
    base = carry_ref[...] + _dot(oh.astype(BF16), before)
    r1 = jnp.sum(oh1 * base, axis=0, keepdims=True)
    r2 = jnp.sum(oh2 * base, axis=0, keepdims=True)
    total = carry_ref[...] + jnp.sum(oh, axis=1, keepdims=True)
    carry_ref[...] = total
    cnt_ref[...] = jnp.broadcast_to(total, cnt_ref.shape)
    route_ref[...] = jnp.concatenate(
        [e1.astype(F32), e2.astype(F32), c1, c2, r1, r2, jnp.zeros((2, tm), F32)], axis=0)


def _router(y, g, w_gr, b_gr, w_er, b_er, *, tm):
    M, D = y.shape
    NE = N_GROUPS * EXP_PER_GROUP
    rows = 32
    wr = jnp.zeros((rows, D), F32).at[:N_GROUPS].set(w_gr.T).at[N_GROUPS:N_GROUPS + NE].set(w_er.T)
    br = jnp.zeros((rows, 1), F32).at[:N_GROUPS, 0].set(b_gr).at[N_GROUPS:N_GROUPS + NE, 0].set(b_er)
    return pl.pallas_call(
        _router_kernel,
        grid=(M // tm,),
        in_specs=[
            pl.BlockSpec((tm, D), lambda i: (i, 0)),
            pl.BlockSpec((1, D), lambda i: (0, 0)),
            pl.BlockSpec((rows, D), lambda i: (0, 0)),
            pl.BlockSpec((rows, 1), lambda i: (0, 0)),
        ],
        out_specs=[
            pl.BlockSpec((tm, D), lambda i: (i, 0)),
            pl.BlockSpec((8, tm), lambda i: (0, i)),
            pl.BlockSpec((NE, 128), lambda i: (0, 0)),
        ],
        out_shape=[
            jax.ShapeDtypeStruct((M, D), BF16),
            jax.ShapeDtypeStruct((8, M), F32),
            jax.ShapeDtypeStruct((NE, 128), F32),
        ],
        scratch_shapes=[pltpu.VMEM((NE, 1), F32)],
        compiler_params=_params(("arbitrary",)),
        name="moe_router",
    )(y, g.reshape(1, D), wr, br)


def _ffn_kernel(te_ref, nu_ref, x_ref, cs_ref, w1_ref, w3_ref, w2_ref, o_ref):
    del te_ref
    i = pl.program_id(0)
    f = pl.program_id(1)
    used = i < nu_ref[0]

    @pl.when(used)
    def _():
        x = x_ref[...]
        a = _dot(x, w1_ref[0].astype(BF16))
        b = _dot(x, w3_ref[0].astype(BF16))
        hh = (a * jax.nn.sigmoid(a)) * b * cs_ref[...]
        y = _dot(hh.astype(BF16), w2_ref[0].astype(BF16))

        @pl.when(f == 0)
        def _():
            o_ref[...] = y

        @pl.when(f > 0)
        def _():
            o_ref[...] += y

    @pl.when(jnp.logical_and(jnp.logical_not(used), f == 0))
    def _():
        o_ref[...] = jnp.zeros_like(o_ref)


def _expert_ffn(xs, cs, tile_expert, n_used, w1, w3, w2, *, tm, tf):
    NS, D = xs.shape
    FF = w1.shape[2]
    nT = NS // tm

    def xi(i, f, te, nu):
        return (jnp.minimum(i, nu[0] - 1), 0)

    return pl.pallas_call(
        _ffn_kernel,
        grid_spec=pltpu.PrefetchScalarGridSpec(
            num_scalar_prefetch=2,
            grid=(nT, FF // tf),
            in_specs=[
                pl.BlockSpec((tm, D), xi),
                pl.BlockSpec((tm, 1), xi),
                pl.BlockSpec((1, D, tf), lambda i, f, te, nu: (te[i], 0, f)),
                pl.BlockSpec((1, D, tf), lambda i, f, te, nu: (te[i], 0, f)),
                pl.BlockSpec((1, tf, D), lambda i, f, te, nu: (te[i], f, 0)),
            ],
            out_specs=pl.BlockSpec((tm, D), lambda i, f, te, nu: (i, 0)),
        ),
        out_shape=jax.ShapeDtypeStruct((NS, D), F32),
        compiler_params=_params(("arbitrary", "arbitrary")),
        name="moe_ffn",
    )(tile_expert, n_used, xs, cs, w1, w3, w2)


def _moe(y, g, w_gr, b_gr, w_er, b_er, w1, w3, w2, *, tm):
    M, D = y.shape
    NE = N_GROUPS * EXP_PER_GROUP
    hb, route, cnt = _router(y, g, w_gr, b_gr, w_er, b_er, tm=tm)
    e1 = route[0].astype(jnp.int32)
    e2 = route[1].astype(jnp.int32)
    c1, c2 = route[2], route[3]
    r1 = route[4].astype(jnp.int32)
    r2 = route[5].astype(jnp.int32)
    counts = cnt[:, 0].astype(jnp.int32)
    padded = ((counts + tm - 1) // tm) * tm
    ends = jnp.cumsum(padded)
    off = ends - padded
    d1 = off[e1] + r1
    d2 = off[e2] + r2
    NS = ((2 * M + NE * (tm - 1)) // tm) * tm
    tok = jnp.arange(M, dtype=jnp.int32)
    src = jnp.zeros((NS,), jnp.int32).at[d1].set(tok).at[d2].set(tok)
    cs = jnp.zeros((NS,), F32).at[d1].set(c1).at[d2].set(c2)
    n_used = (ends[-1] // tm).astype(jnp.int32).reshape(1)
    tile_start = jnp.arange(NS // tm, dtype=jnp.int32) * tm
    tile_start = jnp.minimum(tile_start, ends[-1] - tm)
    tile_expert = jnp.minimum(jnp.searchsorted(ends, tile_start, side="right"), NE - 1).astype(jnp.int32)
    xs = jnp.take(hb, src, axis=0)
    tf = _tile(w1.shape[2], 256)
    ys = _expert_ffn(xs, cs.reshape(NS, 1), tile_expert, n_used, w1, w3, w2, tm=tm, tf=tf)
    return y + (jnp.take(ys, d1, axis=0) + jnp.take(ys, d2, axis=0))


def _latent_kernel(x_ref, gs_ref, w_ref, gc_ref, ckv_ref, kpe_ref):
    x = x_ref[...]
    s = x * _rms_scale(x, x.shape[-1]) * gs_ref[...]
    ck = _dot(s.astype(BF16), w_ref[...])
    lora = ckv_ref.shape[1]
    c = ck[:, :lora]
    ckv_ref[...] = c * _rms_scale(c, lora) * gc_ref[...]
    kpe_ref[...] = ck[:, lora:]


def _latent(y, g_src, w_dkv, g_ckv, *, tm):
    M, D = y.shape
    lora = g_ckv.shape[0]
    N = w_dkv.shape[1]
    return pl.pallas_call(
        _latent_kernel,
        grid=(M // tm,),
        in_specs=[
            pl.BlockSpec((tm, D), lambda i: (i, 0)),
            pl.BlockSpec((1, D), lambda i: (0, 0)),
            pl.BlockSpec((D, N), lambda i: (0, 0)),
            pl.BlockSpec((1, lora), lambda i: (0, 0)),
        ],
        out_specs=[
            pl.BlockSpec((tm, lora), lambda i: (i, 0)),
            pl.BlockSpec((tm, N - lora), lambda i: (i, 0)),
        ],
        out_shape=[
            jax.ShapeDtypeStruct((M, lora), F32),
            jax.ShapeDtypeStruct((M, N - lora), F32),
        ],
        compiler_params=_params(("parallel",)),
        name="shared_latent",
    )(y, g_src.reshape(1, D), w_dkv, g_ckv.reshape(1, lora))


def _head_norm_rope(nope, pe, ssq, gain, cos, sin):
    inv = lax.rsqrt(ssq * (1.0 / QK_DIM) + EPS)
    n = nope * inv * gain[:, :NOPE_DIM]
    p = _rope_half(pe * inv * gain[:, NOPE_DIM:], cos, sin)
    return jnp.concatenate([n, p, jnp.zeros((n.shape[0], QK_PAD - QK_DIM), F32)], axis=-1)


def _kv_prep_kernel(ckv_ref, kpe_ref, wuk_ref, wuv_ref, gk_ref, cos_ref, sin_ref, k_ref, v_ref):
    cb = ckv_ref[...].astype(BF16)
    kn = _dot(cb, wuk_ref[...])
    v_ref[...] = _dot(cb, wuv_ref[...]).astype(v_ref.dtype)
    pe = kpe_ref[...]
    ssq = jnp.sum(kn * kn, axis=-1, keepdims=True) + jnp.sum(pe * pe, axis=-1, keepdims=True)
    k_ref[0] = _head_norm_rope(kn, pe, ssq, gk_ref[...], cos_ref[...], sin_ref[...]).astype(k_ref.dtype)


def _kv_prep(ckv, kpe, w_uk, w_uv, g_k, cos, sin, *, rows, H, tm):
    lora = ckv.shape[1]
    return pl.pallas_call(
        _kv_prep_kernel,
        grid=(rows // tm, H),
        in_specs=[
            pl.BlockSpec((tm, lora), lambda i, h: (i, 0)),
            pl.BlockSpec((tm, ROPE_DIM), lambda i, h: (i, 0)),
            pl.BlockSpec((lora, NOPE_DIM), lambda i, h: (0, h)),
            pl.BlockSpec((lora, V_DIM), lambda i, h: (0, h)),
            pl.BlockSpec((1, QK_DIM), lambda i, h: (0, 0)),
            pl.BlockSpec((tm, ROPE_DIM // 2), lambda i, h: (i, 0)),
            pl.BlockSpec((tm, ROPE_DIM // 2), lambda i, h: (i, 0)),
        ],
        out_specs=[
            pl.BlockSpec((1, tm, QK_PAD), lambda i, h: (h, i, 0)),
            pl.BlockSpec((tm, V_DIM), lambda i, h: (i, h)),
        ],
        out_shape=[
            jax.ShapeDtypeStruct((H, rows, QK_PAD), BF16),
            jax.ShapeDtypeStruct((rows, H * V_DIM), BF16),
        ],
        compiler_params=_params(("parallel", "arbitrary")),
        name="mla_kv_prep",
    )(ckv, kpe, w_uk, w_uv, g_k.reshape(1, QK_DIM), cos, sin)


def _q_prep_kernel(q_ref, gq_ref, cos_ref, sin_ref, o_ref):
    q = q_ref[...]
    ssq = jnp.sum(q * q, axis=-1, keepdims=True)
    o_ref[0] = _head_norm_rope(q[:, :NOPE_DIM], q[:, NOPE_DIM:QK_DIM], ssq, gq_ref[...],
                               cos_ref[...], sin_ref[...]).astype(o_ref.dtype)


def _q_prep(qraw, g_q, cos, sin, *, H, tm):
    M = qraw.shape[0]
    return pl.pallas_call(
        _q_prep_kernel,
        grid=(M // tm, H),
        in_specs=[
            pl.BlockSpec((tm, QK_PAD), lambda i, h: (i, h)),
            pl.BlockSpec((1, QK_DIM), lambda i, h: (0, 0)),
            pl.BlockSpec((tm, ROPE_DIM // 2), lambda i, h: (i, 0)),
            pl.BlockSpec((tm, ROPE_DIM // 2), lambda i, h: (i, 0)),
        ],
        out_specs=pl.BlockSpec((1, tm, QK_PAD), lambda i, h: (h, i, 0)),
        out_shape=jax.ShapeDtypeStruct((H, M, QK_PAD), BF16),
        compiler_params=_params(("parallel", "arbitrary")),
        name="mla_q_prep",
    )(qraw, g_q.reshape(1, QK_DIM), cos, sin)


def _flash_kernel(q_ref, k_ref, v_ref, o_ref, m_ref, l_ref, acc_ref):
    qi = pl.program_id(2)
    ki = pl.program_id(3)
    tq = q_ref.shape[1]
    tk = k_ref.shape[1]

    @pl.when(ki == 0)
    def _():
        m_ref[...] = jnp.full_like(m_ref, -jnp.inf)
        l_ref[...] = jnp.zeros_like(l_ref)
        acc_ref[...] = jnp.zeros_like(acc_ref)

    @pl.when(ki * tk <= qi * tq + (tq - 1))
    def _():
        s = _dot_nt(q_ref[0], k_ref[0]) * (QK_DIM ** -0.5)
        qpos = qi * tq + lax.broadcasted_iota(jnp.int32, (tq, tk), 0)
        kpos = ki * tk + lax.broadcasted_iota(jnp.int32, (tq, tk), 1)
        s = jnp.where(kpos <= qpos, s, -jnp.inf)
        m_old = m_ref[...]
        m_new = jnp.maximum(m_old, jnp.max(s, axis=-1, keepdims=True))
        alpha = jnp.exp(m_old - m_new)
        p = jnp.exp(s - m_new)
        l_ref[...] = alpha * l_ref[...] + jnp.sum(p, axis=-1, keepdims=True)
        acc_ref[...] = alpha * acc_ref[...] + _dot(p.astype(BF16), v_ref[...])
        m_ref[...] = m_new

    @pl.when(ki == pl.num_programs(3) - 1)
    def _():
        o_ref[...] = acc_ref[...] / l_ref[...]


def _flash_prompt(qpad, kpad, v, *, B, T, H, M, tq, tk):
    nq, nk = T // tq, T // tk

    def kidx(qi, ki):
        return jnp.minimum(ki, (qi * tq + tq - 1) // tk)

    return pl.pallas_call(
        _flash_kernel,
        grid=(B, H, nq, nk),
        in_specs=[
            pl.BlockSpec((1, tq, QK_PAD), lambda b, h, qi, ki: (h, b * nq + qi, 0)),
            pl.BlockSpec((1, tk, QK_PAD), lambda b, h, qi, ki: (h, b * nk + kidx(qi, ki), 0)),
            pl.BlockSpec((tk, V_DIM), lambda b, h, qi, ki: (b * nk + kidx(qi, ki), h)),
        ],
        out_specs=pl.BlockSpec((tq, V_DIM), lambda b, h, qi, ki: (b * nq + qi, h)),
        out_shape=jax.ShapeDtypeStruct((M, H * V_DIM), F32),
        scratch_shapes=[pltpu.VMEM((tq, 1), F32), pltpu.VMEM((tq, 1), F32), pltpu.VMEM((tq, V_DIM), F32)],
        compiler_params=_params(("parallel", "parallel", "parallel", "arbitrary")),
        name="mla_prompt_attention",
    )(qpad, kpad, v)


def _absorb_kernel(q_ref, wuk_ref, gk_ref, o_ref):
    qg = q_ref[0][:, :NOPE_DIM].astype(F32) * gk_ref[...]
    o_ref[0] = _dot_nt(qg.astype(BF16), wuk_ref[...]).astype(o_ref.dtype)


def _absorb(qpad, w_uk, g_k, *, H, Mp, Ms):
    lora = w_uk.shape[0]
    assert Mp % Ms == 0
    return pl.pallas_call(
        _absorb_kernel,
        grid=(H,),
        in_specs=[
            pl.BlockSpec((1, Ms, QK_PAD), lambda h: (h, Mp // Ms, 0)),
            pl.BlockSpec((lora, NOPE_DIM), lambda h: (0, h)),
            pl.BlockSpec((1, NOPE_DIM), lambda h: (0, 0)),
        ],
        out_specs=pl.BlockSpec((1, Ms, lora), lambda h: (h, 0, 0)),
        out_shape=jax.ShapeDtypeStruct((H, Ms, lora), BF16),
        compiler_params=_params(("parallel",)),
        name="mla_absorb_queries",
    )(qpad, w_uk, g_k[:NOPE_DIM].reshape(1, NOPE_DIM))


def _decode_kernel(pt_ref, qabs_ref, qpe_ref, wt_ref, *refs, T, H, n_pages):
    del pt_ref
    P = PAGES_PER_STEP
    ckv_pages = refs[:P]
    kpe_pages = refs[P:2 * P]
    cnew_ref, knew_ref, cos_ref, sin_ref, g1_ref, g2_ref, o_ref, cb_ref, kp_ref, m_ref, l_ref, acc_ref = refs[2 * P:]
    b = pl.program_id(0)
    t = pl.program_id(1)
    last = pl.num_programs(1) - 1
    page = ckv_pages[0].shape[1]
    lora = ckv_pages[0].shape[2]
    R = T * H
    past = n_pages * page
    spb = cnew_ref.shape[0] // T

    @pl.when(t == 0)
    def _():
        m_ref[...] = jnp.full_like(m_ref, -jnp.inf)
        l_ref[...] = jnp.zeros_like(l_ref)
        acc_ref[...] = jnp.zeros_like(acc_ref)

    @pl.when(t < last)
    def _():
        for p in range(P):
            cb_ref[p * page:(p + 1) * page, :] = ckv_pages[p][0].astype(BF16)
            kp_ref[p * page:(p + 1) * page, :] = kpe_pages[p][0]

    @pl.when(t == last)
    def _():
        nr = cnew_ref.shape[0]
        cb_ref[0:SUB_POS, :] = jnp.concatenate(
            [cnew_ref[...], jnp.zeros((SUB_POS - nr, lora), F32)], axis=0).astype(BF16)
        kp_ref[0:SUB_POS, :] = jnp.concatenate(
            [knew_ref[...], jnp.zeros((SUB_POS - nr, ROPE_DIM), F32)], axis=0)

    qabs = qabs_ref[0]
    qpe = qpe_ref[0]
    g1 = g1_ref[...]
    g2 = g2_ref[...]
    q_tok = lax.broadcasted_iota(jnp.int32, (R, SUB_POS), 0) // H
    lane = lax.broadcasted_iota(jnp.int32, (R, SUB_POS), 1)

    def body(j, carry):
        start = pl.multiple_of(j * SUB_POS, SUB_POS)
        cb = cb_ref[pl.ds(start, SUB_POS), :]
        kpe = kp_ref[pl.ds(start, SUB_POS), :]
        cos = cos_ref[:, pl.ds(start, SUB_POS)]
        sin = sin_ref[:, pl.ds(start, SUB_POS)]
        kn = _dot_nt(wt_ref[...], cb)
        ssq = jnp.sum((kn * kn).reshape(H, NOPE_DIM, SUB_POS), axis=1)
        sn = _dot_nt(qabs, cb)
        kpe_t = kpe.T
        ssq = ssq + jnp.sum(kpe_t * kpe_t, axis=0, keepdims=True)
        inv = lax.rsqrt(ssq * (1.0 / QK_DIM) + EPS)
        x1 = kpe_t[:ROPE_DIM // 2] * g1
        x2 = kpe_t[ROPE_DIM // 2:] * g2
        kr = jnp.concatenate([x1 * cos - x2 * sin, x1 * sin + x2 * cos], axis=0)
        sr = _dot(qpe, kr.astype(BF16))
        s = (sn + sr) * jnp.concatenate([inv] * T, axis=0) * (QK_DIM ** -0.5)
        pos = t * (P * page) + start + lane
        new_idx = pos - past
        ok_new = jnp.logical_and(new_idx // T == b % spb, new_idx % T <= q_tok)
        ok_new = jnp.logical_and(ok_new, new_idx < spb * T)
        s = jnp.where(jnp.logical_or(pos < past, ok_new), s, -jnp.inf)
        m_old = m_ref[...]
        m_new = jnp.maximum(m_old, jnp.max(s, axis=-1, keepdims=True))
        alpha = jnp.exp(m_old - m_new)
        pr = jnp.exp(s - m_new)
        l_ref[...] = alpha * l_ref[...] + jnp.sum(pr, axis=-1, keepdims=True)
        acc_ref[...] = alpha * acc_ref[...] + _dot(pr.astype(BF16), cb)
        m_ref[...] = m_new
        return carry

    n_sub = jnp.where(t == last, 1, (P * page) // SUB_POS)
    lax.fori_loop(0, n_sub, body, 0)

    @pl.when(t == last)
    def _():
        o_ref[0] = acc_ref[...] / l_ref[...]


def _decode_attention(qabs, qpe, w_uk_t, cache_ckv, cache_kpe, page_table, ckv_all, kpe_all,
                      cos_t, sin_t, g_k, *, Bd, T, H, Mp):
    P = PAGES_PER_STEP
    n_pages = page_table.shape[1]
    page = cache_ckv.shape[1]
    lora = cache_ckv.shape[2]
    assert n_pages % P == 0 and (P * page) % SUB_POS == 0
    steps = n_pages // P + 1
    R = T * H
    NR = 8
    assert NR % T == 0 and Mp % NR == 0
    spb = NR // T
    pt = page_table.reshape(-1).astype(jnp.int32)
    half = ROPE_DIM // 2
    g1 = g_k[NOPE_DIM:NOPE_DIM + half].reshape(half, 1)
    g2 = g_k[NOPE_DIM + half:].reshape(half, 1)

    def page_spec(p, width):
        def idx(b, t, pt):
            return (pt[b * n_pages + jnp.minimum(t, steps - 2) * P + p], 0, 0)
        return pl.BlockSpec((1, page, width), idx)

    in_specs = [
        pl.BlockSpec((1, R, lora), lambda b, t, pt: (b, 0, 0)),
        pl.BlockSpec((1, R, ROPE_DIM), lambda b, t, pt: (b, 0, 0)),
        pl.BlockSpec(w_uk_t.shape, lambda b, t, pt: (0, 0)),
    ]
    in_specs += [page_spec(p, lora) for p in range(P)]
    in_specs += [page_spec(p, ROPE_DIM) for p in range(P)]
    in_specs += [
        pl.BlockSpec((NR, lora), lambda b, t, pt: (Mp // NR + b // spb, 0)),
        pl.BlockSpec((NR, ROPE_DIM), lambda b, t, pt: (Mp // NR + b // spb, 0)),
        pl.BlockSpec((half, P * page), lambda b, t, pt: (0, t)),
        pl.BlockSpec((half, P * page), lambda b, t, pt: (0, t)),
        pl.BlockSpec((half, 1), lambda b, t, pt: (0, 0)),
        pl.BlockSpec((half, 1), lambda b, t, pt: (0, 0)),
    ]
    return pl.pallas_call(
        functools.partial(_decode_kernel, T=T, H=H, n_pages=n_pages),
        grid_spec=pltpu.PrefetchScalarGridSpec(
            num_scalar_prefetch=1,
            grid=(Bd, steps),
            in_specs=in_specs,
            out_specs=pl.BlockSpec((1, R, lora), lambda b, t, pt: (b, 0, 0)),
            scratch_shapes=[
                pltpu.VMEM((P * page, lora), BF16),
                pltpu.VMEM((P * page, ROPE_DIM), F32),
                pltpu.VMEM((R, 1), F32),
                pltpu.VMEM((R, 1), F32),
                pltpu.VMEM((R, lora), F32),
            ],
        ),
        out_shape=jax.ShapeDtypeStruct((Bd, R, lora), F32),
        compiler_params=_params(("parallel", "arbitrary")),
        name="mla_decode_attention",
    )(pt, qabs, qpe, w_uk_t, *([cache_ckv] * P), *([cache_kpe] * P), ckv_all, kpe_all, cos_t, sin_t, g1, g2)


def _head_values_kernel(c_ref, w_ref, alias_ref, o_ref):
    del alias_ref
    o_ref[...] = _dot(c_ref[...].astype(BF16), w_ref[...])


def _head_values(ctx, w_uv, attn, *, H, Mp, Ms):
    lora = w_uv.shape[0]
    return pl.pallas_call(
        _head_values_kernel,
        grid=(H,),
        in_specs=[
            pl.BlockSpec((Ms, lora), lambda h: (0, h)),
            pl.BlockSpec((lora, V_DIM), lambda h: (0, h)),
            pl.BlockSpec(memory_space=pl.ANY),
        ],
        out_specs=pl.BlockSpec((Ms, V_DIM), lambda h: (Mp // Ms, h)),
        out_shape=jax.ShapeDtypeStruct(attn.shape, F32),
        input_output_aliases={2: 0},
        compiler_params=_params(("parallel",)),
        name="mla_decode_values",
    )(ctx, w_uv, attn)


def _rope_tables(pos, d):
    freq = ROPE_THETA ** (-jnp.arange(0, d, 2, dtype=F32) / d)
    ang = pos.astype(F32)[:, None] * freq[None, :]
    return jnp.cos(ang), jnp.sin(ang)


def kernel(x_prompt, x_sample, state_ret, cache_ckv, cache_kpe, page_table, ln_mix, ln_ffn, ret_w_in, ret_w_out,
           kv_src_norm, w_dkv, ckv_norm, w_ukv, k_norm, mla_w_q, q_norm, mla_w_o, moe_w_gr, moe_b_gr, moe_w_er,
           moe_b_er, moe_w1, moe_w3, moe_w2):
    B, T, D = x_prompt.shape
    Bd, Td, _ = x_sample.shape
    n_a = state_ret.shape[0]
    depth = ln_mix.shape[0]
    RH, dk, dv = state_ret.shape[2:]
    lora, MH = w_ukv.shape[0], w_ukv.shape[1]
    Mp, Ms = B * T, Bd * Td
    M = Mp + Ms
    past = page_table.shape[1] * cache_ckv.shape[1]
    tm = _tile(math.gcd(Mp, Ms), 512)

    pos_p = jnp.arange(T)
    pos_s = past + jnp.arange(Td)
    log_g = jnp.log1p(-jnp.exp2(-5.0 - jnp.arange(RH, dtype=F32)))
    y = jnp.concatenate([x_prompt.reshape(Mp, D), x_sample.reshape(Ms, D)], axis=0)

    ret_p, ret_s = [], []
    ckv = kpe = kpad = vals = None
    for i in range(depth):
        if i == n_a:
            ckv, kpe = _latent(y, kv_src_norm, w_dkv.astype(BF16), ckv_norm, tm=tm)
            pos_all = jnp.concatenate([jnp.tile(pos_p, B), jnp.tile(pos_s, Bd)])
            cos_m, sin_m = _rope_tables(pos_all, ROPE_DIM)
            w_uk = w_ukv[..., :NOPE_DIM].reshape(lora, MH * NOPE_DIM).astype(BF16)
            w_uv = w_ukv[..., NOPE_DIM:].reshape(lora, MH * V_DIM).astype(BF16)
            kpad, vals = _kv_prep(ckv, kpe, w_uk, w_uv, k_norm, cos_m, sin_m, rows=Mp, H=MH, tm=tm)
        if i < n_a:
            w_in = ret_w_in[i].astype(BF16)
            qkvg = _matmul(y, w_in, g=ln_mix[i], tm=tm, tn=_tile(w_in.shape[1], 1024), name="ret_in_proj")
            cos_p, sin_p = _rope_tables(pos_p, dk)
            gated, s_p = _retention_prompt(qkvg, log_g, cos_p, sin_p, B=B, T=T, H=RH, dk=dk, dv=dv, M=M)
            cos_s, sin_s = _rope_tables(jnp.tile(pos_s, 8 // Td), dk)
            gated, s_s = _retention_sample(qkvg, log_g, cos_s, sin_s, state_ret[i], gated,
                                           Bd=Bd, T=Td, H=RH, dk=dk, dv=dv, Mp=Mp)
            ret_p.append(s_p)
            ret_s.append(s_s)
            y = _matmul(gated, ret_w_out[i].astype(BF16), res=y, tm=tm, tn=_tile(D, 512), name="ret_out_proj")
        else:
            j = i - n_a
            wq = mla_w_q[j].reshape(D, MH, QK_DIM)
            wq = jnp.pad(wq, ((0, 0), (0, 0), (0, QK_PAD - QK_DIM))).reshape(D, MH * QK_PAD).astype(BF16)
            qraw = _matmul(y, wq, g=ln_mix[i], tm=tm, tn=_tile(MH * QK_PAD, 1024), name="mla_q_proj")
            qpad = _q_prep(qraw, q_norm[j], cos_m, sin_m, H=MH, tm=tm)
            attn = _flash_prompt(qpad, kpad, vals, B=B, T=T, H=MH, M=M, tq=_tile(T, 512), tk=_tile(T, 512))
            qabs = _absorb(qpad, w_uk, k_norm, H=MH, Mp=Mp, Ms=Ms)
            qabs = qabs.reshape(MH, Bd, Td, lora).transpose(1, 2, 0, 3).reshape(Bd, Td * MH, lora)
            qpe = qpad[:, Mp:, NOPE_DIM:QK_DIM].reshape(MH, Bd, Td, ROPE_DIM)
            qpe = qpe.transpose(1, 2, 0, 3).reshape(Bd, Td * MH, ROPE_DIM)
            span = PAGES_PER_STEP * cache_ckv.shape[1]
            n_cols = past + span
            col = jnp.arange(n_cols)
            pos_cols = jnp.where(col < past, col, past + (col - past) % Td)
            cos_c, sin_c = _rope_tables(pos_cols, ROPE_DIM)
            w_uk_t = w_uk.T
            ctx = _decode_attention(qabs, qpe, w_uk_t, cache_ckv, cache_kpe, page_table, ckv, kpe,
                                    cos_c.T, sin_c.T, k_norm, Bd=Bd, T=Td, H=MH, Mp=Mp)
            attn = _head_values(ctx.reshape(Ms, MH * lora), w_uv, attn, H=MH, Mp=Mp, Ms=Ms)
            y = _matmul(attn, mla_w_o[j].astype(BF16), res=y, tm=tm, tn=_tile(D, 512), name="mla_out_proj")
        y = _moe(y, ln_ffn[i], moe_w_gr[i], moe_b_gr[i], moe_w_er[i], moe_b_er[i],
                 moe_w1[i], moe_w3[i], moe_w2[i], tm=tm)

    yp = y[:Mp].reshape(B, T, D)
    ys = y[Mp:].reshape(Bd, Td, D)
    return (yp, ys, jnp.stack(ret_p, axis=0), jnp.stack(ret_s, axis=0),
            ckv[:Mp].reshape(B, T, lora), kpe[:Mp].reshape(B, T, ROPE_DIM),
            ckv[Mp:].reshape(Bd, Td, lora), kpe[Mp:].reshape(Bd, Td, ROPE_DIM))
```

```python
import functools
import math

import jax
import jax.numpy as jnp
from jax import lax
from jax.experimental import pallas as pl
from jax.experimental.pallas import tpu as pltpu

F32 = jnp.float32
BF16 = jnp.bfloat16
EPS = 1e-6
ROPE_THETA = 10000.0
RET_CHUNK = 128
NOPE_DIM = 128
ROPE_DIM = 64
V_DIM = 128
QK_DIM = NOPE_DIM + ROPE_DIM
QK_PAD = 256
N_GROUPS = 4
EXP_PER_GROUP = 4
PAGES_PER_STEP = 16
SUB_POS = 512
MIB = 1024 * 1024


def _params(sem, vmem_mib=48):
    return pltpu.CompilerParams(dimension_semantics=sem, vmem_limit_bytes=vmem_mib * MIB)


def _tile(n, pref):
    t = math.gcd(n, pref)
    assert t % 8 == 0 or t == n, (n, pref)
    return t


def _dot(a, b):
    return jnp.dot(a, b, preferred_element_type=F32)


def _dot_nt(a, b):
    return lax.dot_general(a, b, (((1,), (1,)), ((), ())), preferred_element_type=F32)


def _rms_scale(x, n):
    return lax.rsqrt(jnp.sum(x * x, axis=-1, keepdims=True) * (1.0 / n) + EPS)


def _rope_half(x, cos, sin):
    half = x.shape[-1] // 2
    x1, x2 = x[:, :half], x[:, half:]
    return jnp.concatenate([x1 * cos - x2 * sin, x1 * sin + x2 * cos], axis=-1)


def _mm_kernel(*refs, has_norm, has_res):
    it = iter(refs)
    x_ref = next(it)
    g_ref = next(it) if has_norm else None
    w_ref = next(it)
    r_ref = next(it) if has_res else None
    o_ref = next(it)
    xb_ref = next(it)

    @pl.when(pl.program_id(1) == 0)
    def _():
        x = x_ref[...].astype(F32)
        if has_norm:
            x = x * _rms_scale(x, x.shape[-1]) * g_ref[...]
        xb_ref[...] = x.astype(BF16)

    acc = _dot(xb_ref[...], w_ref[...])
    if has_res:
        acc = r_ref[...] + acc
    o_ref[...] = acc.astype(o_ref.dtype)


def _matmul(x, w, *, g=None, res=None, rows=None, tm, tn, out_dtype=F32, name):
    M, K = x.shape
    N = w.shape[1]
    rows = M if rows is None else rows
    in_specs = [pl.BlockSpec((tm, K), lambda i, j: (i, 0))]
    args = [x]
    if g is not None:
        in_specs.append(pl.BlockSpec((1, K), lambda i, j: (0, 0)))
        args.append(g.reshape(1, K))
    in_specs.append(pl.BlockSpec((K, tn), lambda i, j: (0, j)))
    args.append(w)
    if res is not None:
        in_specs.append(pl.BlockSpec((tm, tn), lambda i, j: (i, j)))
        args.append(res)
    return pl.pallas_call(
        functools.partial(_mm_kernel, has_norm=g is not None, has_res=res is not None),
        grid=(rows // tm, N // tn),
        in_specs=in_specs,
        out_specs=pl.BlockSpec((tm, tn), lambda i, j: (i, j)),
        out_shape=jax.ShapeDtypeStruct((rows, N), out_dtype),
        scratch_shapes=[pltpu.VMEM((tm, K), BF16)],
        compiler_params=_params(("parallel", "arbitrary")),
        name=name,
    )(*args)


def _ret_prompt_kernel(lg_ref, q_ref, k_ref, v_ref, g_ref, cos_ref, sin_ref, o_ref, s_out_ref, s_ref):
    h = pl.program_id(1)
    c = pl.program_id(2)
    lg = lg_ref[h]
    L, dk = q_ref.shape

    @pl.when(c == 0)
    def _():
        s_ref[...] = jnp.zeros_like(s_ref)

    cos, sin = cos_ref[...], sin_ref[...]
    q = _rope_half(q_ref[...], cos, sin)
    k = _rope_half(k_ref[...], cos, sin) * (dk ** -0.5)
    vb = v_ref[...].astype(BF16)
    qb = q.astype(BF16)
    ii = lax.broadcasted_iota(jnp.int32, (L, L), 0)
    jj = lax.broadcasted_iota(jnp.int32, (L, L), 1)
    rel = (ii - jj).astype(F32)
    decay = jnp.where(rel >= 0, jnp.exp(lg * jnp.maximum(rel, 0.0)), 0.0)
    scores = _dot_nt(qb, k.astype(BF16)) * decay
    inner = _dot(scores.astype(BF16), vb)
    s_old = s_ref[...]
    ri = lax.broadcasted_iota(jnp.int32, (L, 1), 0).astype(F32)
    cross = _dot(qb, s_old.astype(BF16)) * jnp.exp((ri + 1.0) * lg)
    k_dec = k * jnp.exp((L - 1.0 - ri) * lg)
    g_all = jnp.exp(jnp.zeros((1, 1), F32) + L * lg)
    s_new = g_all * s_old + _dot(k_dec.T.astype(BF16), vb)
    s_ref[...] = s_new
    o = inner + cross
    o = o * _rms_scale(o, o.shape[-1])
    gate = g_ref[...]
    o_ref[...] = (gate * jax.nn.sigmoid(gate)) * o

    @pl.when(c == pl.num_programs(2) - 1)
    def _():
        s_out_ref[0, 0] = s_new


def _retention_prompt(qkvg, log_g, cos, sin, *, B, T, H, dk, dv, M):
    C = RET_CHUNK if T % RET_CHUNK == 0 else T
    nC = T // C
    assert (2 * H * dk) % dv == 0
    voff = 2 * H * dk // dv
    return pl.pallas_call(
        _ret_prompt_kernel,
        grid=(B, H, nC),
        in_specs=[
            pl.BlockSpec(memory_space=pltpu.SMEM),
            pl.BlockSpec((C, dk), lambda b, h, c: (b * nC + c, h)),
            pl.BlockSpec((C, dk), lambda b, h, c: (b * nC + c, H + h)),
            pl.BlockSpec((C, dv), lambda b, h, c: (b * nC + c, voff + h)),
            pl.BlockSpec((C, dv), lambda b, h, c: (b * nC + c, voff + H + h)),
            pl.BlockSpec((C, dk // 2), lambda b, h, c: (c, 0)),
            pl.BlockSpec((C, dk // 2), lambda b, h, c: (c, 0)),
        ],
        out_specs=[
            pl.BlockSpec((C, dv), lambda b, h, c: (b * nC + c, h)),
            pl.BlockSpec((1, 1, dk, dv), lambda b, h, c: (b, h, 0, 0)),
        ],
        out_shape=[
            jax.ShapeDtypeStruct((M, H * dv), F32),
            jax.ShapeDtypeStruct((B, H, dk, dv), F32),
        ],
        scratch_shapes=[pltpu.VMEM((dk, dv), F32)],
        compiler_params=_params(("parallel", "parallel", "arbitrary")),
        name="retention_prompt",
    )(log_g, qkvg, qkvg, qkvg, qkvg, cos, sin)


def _ret_sample_kernel(lg_ref, q_ref, k_ref, v_ref, g_ref, cos_ref, sin_ref, s_in_ref, o_alias_ref,
                       o_ref, s_out_ref, *, T):
    del o_alias_ref
    hg = pl.program_id(1)
    R = q_ref.shape[0]
    HB, dk, dv = s_in_ref.shape[1:]
    cos, sin = cos_ref[...], sin_ref[...]
    ii = lax.broadcasted_iota(jnp.int32, (R, R), 0)
    jj = lax.broadcasted_iota(jnp.int32, (R, R), 1)
    rel = (ii - jj).astype(F32)
    same = (ii // T) == (jj // T)
    row = lax.broadcasted_iota(jnp.int32, (R, 1), 0)
    tok = (row % T).astype(F32)
    pad = RET_CHUNK - R
    for hh in range(HB):
        lg = lg_ref[hg * HB + hh]
        q = _rope_half(q_ref[:, hh * dk:(hh + 1) * dk], cos, sin)
        k = _rope_half(k_ref[:, hh * dk:(hh + 1) * dk], cos, sin) * (dk ** -0.5)
        v = v_ref[:, hh * dv:(hh + 1) * dv]
        qb = q.astype(BF16)
        decay = jnp.where(same, jnp.where(rel >= 0, jnp.exp(lg * jnp.maximum(rel, 0.0)), 0.0), 0.0)
        scores = _dot_nt(qb, k.astype(BF16)) * decay
        sc = scores.astype(BF16).astype(F32)
        vr = v.astype(BF16).astype(F32)
        inner = jnp.zeros((R, dv), F32)
        for j in range(R):
            inner = inner + sc[:, j:j + 1] * vr[j:j + 1, :]
        k_dec = k * jnp.exp((T - 1.0 - tok) * lg)
        g_all = jnp.exp(jnp.zeros((1, 1), F32) + T * lg)
        v_pad = jnp.concatenate([v, jnp.zeros((pad, dv), F32)], axis=0).astype(BF16)
        cross = jnp.zeros((R, dv), F32)
        for s in range(R // T):
            mine = (row // T) == s
            s_old = s_in_ref[s, hh]
            cross = jnp.where(mine, _dot(qb, s_old.astype(BF16)), cross)
            kd = jnp.where(mine, k_dec, 0.0)
            kd_t = jnp.concatenate([kd, jnp.zeros((pad, dk), F32)], axis=0).T.astype(BF16)
            s_out_ref[s, hh] = g_all * s_old + _dot(kd_t, v_pad)
        o = inner + cross * jnp.exp((tok + 1.0) * lg)
        o = o * _rms_scale(o, dv)
        gate = g_ref[:, hh * dv:(hh + 1) * dv]
        o_ref[:, hh * dv:(hh + 1) * dv] = (gate * jax.nn.sigmoid(gate)) * o


def _retention_sample(qkvg, log_g, cos, sin, state, gated, *, Bd, T, H, dk, dv, Mp):
    R = 8
    assert R % T == 0 and Bd % (R // T) == 0 and Mp % R == 0
    spb = R // T
    rb0 = Mp // R
    HB = math.gcd(H, 4)
    assert (2 * H * dk) % (HB * dv) == 0
    nH = H // HB
    voff = 2 * H * dk // (HB * dv)
    M = qkvg.shape[0]
    return pl.pallas_call(
        functools.partial(_ret_sample_kernel, T=T),
        grid=(Bd // spb, nH),
        in_specs=[
            pl.BlockSpec(memory_space=pltpu.SMEM),
            pl.BlockSpec((R, HB * dk), lambda i, h: (rb0 + i, h)),
            pl.BlockSpec((R, HB * dk), lambda i, h: (rb0 + i, nH + h)),
            pl.BlockSpec((R, HB * dv), lambda i, h: (rb0 + i, voff + h)),
            pl.BlockSpec((R, HB * dv), lambda i, h: (rb0 + i, voff + nH + h)),
            pl.BlockSpec((R, dk // 2), lambda i, h: (0, 0)),
            pl.BlockSpec((R, dk // 2), lambda i, h: (0, 0)),
            pl.BlockSpec((spb, HB, dk, dv), lambda i, h: (i, h, 0, 0)),
            pl.BlockSpec(memory_space=pl.ANY),
        ],
        out_specs=[
            pl.BlockSpec((R, HB * dv), lambda i, h: (rb0 + i, h)),
            pl.BlockSpec((spb, HB, dk, dv), lambda i, h: (i, h, 0, 0)),
        ],
        out_shape=[
            jax.ShapeDtypeStruct((M, H * dv), F32),
            jax.ShapeDtypeStruct((Bd, H, dk, dv), F32),
        ],
        input_output_aliases={8: 0},
        compiler_params=_params(("parallel", "parallel")),
        name="retention_sample",
    )(log_g, qkvg, qkvg, qkvg, qkvg, cos, sin, state, gated)


def _router_kernel(y_ref, g_ref, wr_ref, br_ref, route_ref, cnt_ref, carry_ref):
    i = pl.program_id(0)
    tm = y_ref.shape[0]
    NE = N_GROUPS * EXP_PER_GROUP

    @pl.when(i == 0)
    def _():
        carry_ref[...] = jnp.zeros_like(carry_ref)

    y = y_ref[...]
    hn = y * _rms_scale(y, y.shape[-1]) * g_ref[...]
    lt = lax.dot_general(wr_ref[...], hn, (((1,), (1,)), ((), ())),
                         precision=lax.Precision.HIGHEST, preferred_element_type=F32) + br_ref[...]
    gl = [lt[k:k + 1, :] for k in range(N_GROUPS)]
    gmax = functools.reduce(jnp.maximum, gl)
    grp = jnp.full(gmax.shape, N_GROUPS - 1, jnp.int32)
    for k in range(N_GROUPS - 2, -1, -1):
        grp = jnp.where(gl[k] == gmax, k, grp)
    pg = 1.0 / functools.reduce(lambda a, b: a + b, [jnp.exp(x - gmax) for x in gl])
    el = []
    for k in range(EXP_PER_GROUP):
        sel = lt[N_GROUPS + (N_GROUPS - 1) * EXP_PER_GROUP + k:N_GROUPS + (N_GROUPS - 1) * EXP_PER_GROUP + k + 1, :]
        for gi in range(N_GROUPS - 2, -1, -1):
            r = N_GROUPS + gi * EXP_PER_GROUP + k
            sel = jnp.where(grp == gi, lt[r:r + 1, :], sel)
        el.append(sel)
    v1 = functools.reduce(jnp.maximum, el)
    i1 = jnp.full(v1.shape, EXP_PER_GROUP - 1, jnp.int32)
    for k in range(EXP_PER_GROUP - 2, -1, -1):
        i1 = jnp.where(el[k] == v1, k, i1)
    el2 = [jnp.where(i1 == k, -jnp.inf, el[k]) for k in range(EXP_PER_GROUP)]
    v2 = functools.reduce(jnp.maximum, el2)
    i2 = jnp.full(v2.shape, EXP_PER_GROUP - 1, jnp.int32)
    for k in range(EXP_PER_GROUP - 2, -1, -1):
        i2 = jnp.where(el2[k] == v2, k, i2)
    e21 = jnp.exp(v2 - v1)
    den = 1.0 / (1.0 + e21)
    c1 = pg * den
    c2 = pg * (e21 * den)
    e1 = grp * EXP_PER_GROUP + i1
    e2 = grp * EXP_PER_GROUP + i2
    eid = lax.broadcasted_iota(jnp.int32, (NE, tm), 0)
    oh1 = (eid == e1).astype(F32)
    oh2 = (eid == e2).astype(F32)
    oh = oh1 + oh2
    tr = lax.broadcasted_iota(jnp.int32, (tm, tm), 0)
    tc = lax.broadcasted_iota(jnp.int32, (tm, tm), 1)
    before = jnp.where(tr < tc, 1.0, 0.0).astype(BF16)
    base = carry_ref[...] + _dot(oh.astype(BF16), before)
    r1 = jnp.sum(oh1 * base, axis=0, keepdims=True)
    r2 = jnp.sum(oh2 * base, axis=0, keepdims=True)
    total = carry_ref[...] + jnp.sum(oh, axis=1, keepdims=True)
    carry_ref[...] = total
    cnt_ref[...] = jnp.broadcast_to(total, cnt_ref.shape)
    route_ref[...] = jnp.concatenate(
        [e1.astype(F32), e2.astype(F32), c1, c2, r1, r2, jnp.zeros((2, tm), F32)], axis=0)


def _router(y, g, w_gr, b_gr, w_er, b_er, *, tm):
    M, D = y.shape
    NE = N_GROUPS * EXP_PER_GROUP
    rows = 32
    wr = jnp.zeros((rows, D), F32).at[:N_GROUPS].set(w_gr.T).at[N_GROUPS:N_GROUPS + NE].set(w_er.T)
    br = jnp.zeros((rows, 1), F32).at[:N_GROUPS, 0].set(b_gr).at[N_GROUPS:N_GROUPS + NE, 0].set(b_er)
    return pl.pallas_call(
        _router_kernel,
        grid=(M // tm,),
        in_specs=[
            pl.BlockSpec((tm, D), lambda i: (i, 0)),
            pl.BlockSpec((1, D), lambda i: (0, 0)),
            pl.BlockSpec((rows, D), lambda i: (0, 0)),
            pl.BlockSpec((rows, 1), lambda i: (0, 0)),
        ],
        out_specs=[
            pl.BlockSpec((8, tm), lambda i: (0, i)),
            pl.BlockSpec((NE, 128), lambda i: (0, 0)),
        ],
        out_shape=[
            jax.ShapeDtypeStruct((8, M), F32),
            jax.ShapeDtypeStruct((NE, 128), F32),
        ],
        scratch_shapes=[pltpu.VMEM((NE, 1), F32)],
        compiler_params=_params(("arbitrary",)),
        name="moe_router",
    )(y, g.reshape(1, D), wr, br)


def _ffn_kernel(te_ref, nu_ref, x_ref, g_ref, w1_ref, w3_ref, w2_ref, o_ref, xb_ref):
    del te_ref
    i = pl.program_id(0)
    f = pl.program_id(1)
    used = i < nu_ref[0]

    @pl.when(jnp.logical_and(used, f == 0))
    def _():
        x = x_ref[...]
        xb_ref[...] = (x * _rms_scale(x, x.shape[-1]) * g_ref[...]).astype(BF16)

    @pl.when(used)
    def _():
        x = xb_ref[...]
        a = _dot(x, w1_ref[0, 0].astype(BF16))
        b = _dot(x, w3_ref[0, 0].astype(BF16))
        hh = (a * jax.nn.sigmoid(a)) * b
        y = _dot(hh.astype(BF16), w2_ref[0, 0].astype(BF16))

        @pl.when(f == 0)
        def _():
            o_ref[...] = y

        @pl.when(f > 0)
        def _():
            o_ref[...] += y

    @pl.when(jnp.logical_and(jnp.logical_not(used), f == 0))
    def _():
        o_ref[...] = jnp.zeros_like(o_ref)


def _expert_ffn(xs, g, tile_expert, n_used, w1, w3, w2, *, layer, tm, tf):
    NS, D = xs.shape
    FF = w1.shape[3]
    nT = NS // tm

    def xi(i, f, te, nu):
        return (jnp.minimum(i, nu[0] - 1), 0)

    return pl.pallas_call(
        _ffn_kernel,
        grid_spec=pltpu.PrefetchScalarGridSpec(
            num_scalar_prefetch=2,
            grid=(nT, FF // tf),
            in_specs=[
                pl.BlockSpec((tm, D), xi),
                pl.BlockSpec((1, D), lambda i, f, te, nu: (0, 0)),
                pl.BlockSpec((1, 1, D, tf), lambda i, f, te, nu: (layer, te[i], 0, f)),
                pl.BlockSpec((1, 1, D, tf), lambda i, f, te, nu: (layer, te[i], 0, f)),
                pl.BlockSpec((1, 1, tf, D), lambda i, f, te, nu: (layer, te[i], f, 0)),
            ],
            out_specs=pl.BlockSpec((tm, D), lambda i, f, te, nu: (i, 0)),
            scratch_shapes=[pltpu.VMEM((tm, D), BF16)],
        ),
        out_shape=jax.ShapeDtypeStruct((NS, D), F32),
        compiler_params=_params(("arbitrary", "arbitrary")),
        name="moe_ffn",
    )(tile_expert, n_used, xs, g.reshape(1, D), w1, w3, w2)


def _combine_kernel(y_ref, a_ref, b_ref, c_ref, o_ref):
    c = c_ref[...]
    o_ref[...] = y_ref[...] + (c[:, 0:1] * a_ref[...] + c[:, 1:2] * b_ref[...])


def _combine(y, a, b, c, *, tm):
    M, D = y.shape
    row = pl.BlockSpec((tm, D), lambda i: (i, 0))
    return pl.pallas_call(
        _combine_kernel,
        grid=(M // tm,),
        in_specs=[row, row, row, pl.BlockSpec((tm, 2), lambda i: (i, 0))],
        out_specs=row,
        out_shape=jax.ShapeDtypeStruct((M, D), F32),
        compiler_params=_params(("parallel",)),
        name="moe_combine",
    )(y, a, b, c)


def _moe(y, g, w_gr, b_gr, w_er, b_er, w1, w3, w2, *, layer, tm):
    M, D = y.shape
    NE = N_GROUPS * EXP_PER_GROUP
    route, cnt = _router(y, g, w_gr, b_gr, w_er, b_er, tm=tm)
    e1 = route[0].astype(jnp.int32)
    e2 = route[1].astype(jnp.int32)
    r1 = route[4].astype(jnp.int32)
    r2 = route[5].astype(jnp.int32)
    counts = cnt[:, 0].astype(jnp.int32)
    padded = ((counts + tm - 1) // tm) * tm
    ends = jnp.cumsum(padded)
    off = ends - padded
    d1 = off[e1] + r1
    d2 = off[e2] + r2
    NS = ((2 * M + NE * (tm - 1)) // tm) * tm
    tok = jnp.arange(M, dtype=jnp.int32)
    src = jnp.zeros((NS,), jnp.int32).at[jnp.concatenate([d1, d2])].set(
        jnp.concatenate([tok, tok]), unique_indices=True, mode="promise_in_bounds")
    n_used = (ends[-1] // tm).astype(jnp.int32).reshape(1)
    tile_start = jnp.minimum(jnp.arange(NS // tm, dtype=jnp.int32) * tm, ends[-1] - tm)
    tile_expert = jnp.sum((ends[None, :] <= tile_start[:, None]).astype(jnp.int32), axis=1)
    tile_expert = jnp.minimum(tile_expert, NE - 1)
    xs = y.at[src].get(mode="promise_in_bounds")
    tf = _tile(w1.shape[3], 256)
    ys = _expert_ffn(xs, g, tile_expert, n_used, w1, w3, w2, layer=layer, tm=tm, tf=tf)
    a = ys.at[d1].get(mode="promise_in_bounds")
    b = ys.at[d2].get(mode="promise_in_bounds")
    return _combine(y, a, b, route[2:4].T, tm=tm)


def _latent_kernel(x_ref, gs_ref, w_ref, gc_ref, ckv_ref, kpe_ref):
    x = x_ref[...]
    s = x * _rms_scale(x, x.shape[-1]) * gs_ref[...]
    ck = _dot(s.astype(BF16), w_ref[...])
    lora = ckv_ref.shape[1]
    c = ck[:, :lora]
    ckv_ref[...] = c * _rms_scale(c, lora) * gc_ref[...]
    kpe_ref[...] = ck[:, lora:]


def _latent(y, g_src, w_dkv, g_ckv, *, tm):
    M, D = y.shape
    lora = g_ckv.shape[0]
    N = w_dkv.shape[1]
    return pl.pallas_call(
        _latent_kernel,
        grid=(M // tm,),
        in_specs=[
            pl.BlockSpec((tm, D), lambda i: (i, 0)),
            pl.BlockSpec((1, D), lambda i: (0, 0)),
            pl.BlockSpec((D, N), lambda i: (0, 0)),
            pl.BlockSpec((1, lora), lambda i: (0, 0)),
        ],
        out_specs=[
            pl.BlockSpec((tm, lora), lambda i: (i, 0)),
            pl.BlockSpec((tm, N - lora), lambda i: (i, 0)),
        ],
        out_shape=[
            jax.ShapeDtypeStruct((M, lora), F32),
            jax.ShapeDtypeStruct((M, N - lora), F32),
        ],
        compiler_params=_params(("parallel",)),
        name="shared_latent",
    )(y, g_src.reshape(1, D), w_dkv, g_ckv.reshape(1, lora))


def _head_norm_rope(nope, pe, ssq, gain, cos, sin):
    inv = lax.rsqrt(ssq * (1.0 / QK_DIM) + EPS)
    n = nope * inv * gain[:, :NOPE_DIM]
    p = _rope_half(pe * inv * gain[:, NOPE_DIM:], cos, sin)
    return jnp.concatenate([n, p, jnp.zeros((n.shape[0], QK_PAD - QK_DIM), F32)], axis=-1)


def _kv_prep_kernel(ckv_ref, kpe_ref, wuk_ref, wuv_ref, gk_ref, cos_ref, sin_ref, k_ref, v_ref):
    cb = ckv_ref[...].astype(BF16)
    kn = _dot(cb, wuk_ref[...])
    v_ref[...] = _dot(cb, wuv_ref[...]).astype(v_ref.dtype)
    pe = kpe_ref[...]
    ssq = jnp.sum(kn * kn, axis=-1, keepdims=True) + jnp.sum(pe * pe, axis=-1, keepdims=True)
    k_ref[0] = _head_norm_rope(kn, pe, ssq, gk_ref[...], cos_ref[...], sin_ref[...]).astype(k_ref.dtype)


def _kv_prep(ckv, kpe, w_uk, w_uv, g_k, cos, sin, *, rows, H, tm):
    lora = ckv.shape[1]
    return pl.pallas_call(
        _kv_prep_kernel,
        grid=(rows // tm, H),
        in_specs=[
            pl.BlockSpec((tm, lora), lambda i, h: (i, 0)),
            pl.BlockSpec((tm, ROPE_DIM), lambda i, h: (i, 0)),
            pl.BlockSpec((lora, NOPE_DIM), lambda i, h: (0, h)),
            pl.BlockSpec((lora, V_DIM), lambda i, h: (0, h)),
            pl.BlockSpec((1, QK_DIM), lambda i, h: (0, 0)),
            pl.BlockSpec((tm, ROPE_DIM // 2), lambda i, h: (i, 0)),
            pl.BlockSpec((tm, ROPE_DIM // 2), lambda i, h: (i, 0)),
        ],
        out_specs=[
            pl.BlockSpec((1, tm, QK_PAD), lambda i, h: (h, i, 0)),
            pl.BlockSpec((tm, V_DIM), lambda i, h: (i, h)),
        ],
        out_shape=[
            jax.ShapeDtypeStruct((H, rows, QK_PAD), BF16),
            jax.ShapeDtypeStruct((rows, H * V_DIM), BF16),
        ],
        compiler_params=_params(("parallel", "arbitrary")),
        name="mla_kv_prep",
    )(ckv, kpe, w_uk, w_uv, g_k.reshape(1, QK_DIM), cos, sin)


def _q_prep_kernel(q_ref, gq_ref, cos_ref, sin_ref, o_ref):
    q = q_ref[...]
    ssq = jnp.sum(q * q, axis=-1, keepdims=True)
    o_ref[0] = _head_norm_rope(q[:, :NOPE_DIM], q[:, NOPE_DIM:QK_DIM], ssq, gq_ref[...],
                               cos_ref[...], sin_ref[...]).astype(o_ref.dtype)


def _q_prep(qraw, g_q, cos, sin, *, H, tm):
    M = qraw.shape[0]
    return pl.pallas_call(
        _q_prep_kernel,
        grid=(M // tm, H),
        in_specs=[
            pl.BlockSpec((tm, QK_PAD), lambda i, h: (i, h)),
            pl.BlockSpec((1, QK_DIM), lambda i, h: (0, 0)),
            pl.BlockSpec((tm, ROPE_DIM // 2), lambda i, h: (i, 0)),
            pl.BlockSpec((tm, ROPE_DIM // 2), lambda i, h: (i, 0)),
        ],
        out_specs=pl.BlockSpec((1, tm, QK_PAD), lambda i, h: (h, i, 0)),
        out_shape=jax.ShapeDtypeStruct((H, M, QK_PAD), BF16),
        compiler_params=_params(("parallel", "arbitrary")),
        name="mla_q_prep",
    )(qraw, g_q.reshape(1, QK_DIM), cos, sin)


def _flash_kernel(q_ref, k_ref, v_ref, o_ref, m_ref, l_ref, acc_ref):
    qi = pl.program_id(2)
    HB, tq, _ = q_ref.shape
    scale = QK_DIM ** -0.5
    m_ref[...] = jnp.full_like(m_ref, -jnp.inf)
    l_ref[...] = jnp.zeros_like(l_ref)
    acc_ref[...] = jnp.zeros_like(acc_ref)

    def step(start, masked):
        for h in range(HB):
            k = k_ref[h, pl.ds(start, tq), :]
            v = v_ref[pl.ds(start, tq), h * V_DIM:(h + 1) * V_DIM]
            s = _dot_nt(q_ref[h], k) * scale
            if masked:
                row = lax.broadcasted_iota(jnp.int32, (tq, tq), 0)
                col = lax.broadcasted_iota(jnp.int32, (tq, tq), 1)
                s = jnp.where(col <= row, s, -jnp.inf)
            m_old = m_ref[h]
            m_new = jnp.maximum(m_old, jnp.max(s, axis=-1, keepdims=True))
            alpha = jnp.exp(m_old - m_new)
            p = jnp.exp(s - m_new)
            l_ref[h] = alpha * l_ref[h] + jnp.sum(p, axis=-1, keepdims=True)
            acc_ref[h] = alpha * acc_ref[h] + _dot(p.astype(BF16), v)
            m_ref[h] = m_new

    def full_tile(ki, carry):
        step(pl.multiple_of(ki * tq, tq), False)
        return carry

    lax.fori_loop(0, qi, full_tile, 0)
    step(pl.multiple_of(qi * tq, tq), True)
    for h in range(HB):
        o_ref[:, h * V_DIM:(h + 1) * V_DIM] = acc_ref[h] / l_ref[h]


def _flash_prompt(qpad, kpad, v, *, B, T, H, M, tq):
    nq = T // tq
    HB = math.gcd(H, 2)
    return pl.pallas_call(
        _flash_kernel,
        grid=(B, H // HB, nq),
        in_specs=[
            pl.BlockSpec((HB, tq, QK_PAD), lambda b, h, qi: (h, b * nq + qi, 0)),
            pl.BlockSpec((HB, T, QK_PAD), lambda b, h, qi: (h, b, 0)),
            pl.BlockSpec((T, HB * V_DIM), lambda b, h, qi: (b, h)),
        ],
        out_specs=pl.BlockSpec((tq, HB * V_DIM), lambda b, h, qi: (b * nq + qi, h)),
        out_shape=jax.ShapeDtypeStruct((M, H * V_DIM), F32),
        scratch_shapes=[pltpu.VMEM((HB, tq, 1), F32), pltpu.VMEM((HB, tq, 1), F32),
                        pltpu.VMEM((HB, tq, V_DIM), F32)],
        compiler_params=_params(("parallel", "parallel", "arbitrary")),
        name="mla_prompt_attention",
    )(qpad, kpad, v)


def _absorb_kernel(q_ref, wuk_ref, gk_ref, o_ref):
    qg = q_ref[0][:, :NOPE_DIM].astype(F32) * gk_ref[...]
    o_ref[0] = _dot_nt(qg.astype(BF16), wuk_ref[...]).astype(o_ref.dtype)


def _absorb(qpad, w_uk, g_k, *, H, Mp, Ms):
    lora = w_uk.shape[0]
    assert Mp % Ms == 0
    return pl.pallas_call(
        _absorb_kernel,
        grid=(H,),
        in_specs=[
            pl.BlockSpec((1, Ms, QK_PAD), lambda h: (h, Mp // Ms, 0)),
            pl.BlockSpec((lora, NOPE_DIM), lambda h: (0, h)),
            pl.BlockSpec((1, NOPE_DIM), lambda h: (0, 0)),
        ],
        out_specs=pl.BlockSpec((1, Ms, lora), lambda h: (h, 0, 0)),
        out_shape=jax.ShapeDtypeStruct((H, Ms, lora), BF16),
        compiler_params=_params(("parallel",)),
        name="mla_absorb_queries",
    )(qpad, w_uk, g_k[:NOPE_DIM].reshape(1, NOPE_DIM))


def _decode_kernel(pt_ref, qabs_ref, qpe_ref, wt_ref, *refs, T, H):
    del pt_ref
    P = PAGES_PER_STEP
    ckv_pages = refs[:P]
    kpe_pages = refs[P:2 * P]
    cnew_ref, knew_ref, cos_ref, sin_ref, g1_ref, g2_ref, o_ref, m_ref, l_ref, acc_ref = refs[2 * P:]
    b = pl.program_id(0)
    t = pl.program_id(1)
    last = pl.num_programs(1) - 1
    page = ckv_pages[0].shape[1]
    lora = ckv_pages[0].shape[2]
    R = T * H
    NR = cnew_ref.shape[0]
    pps = SUB_POS // page

    @pl.when(t == 0)
    def _():
        m_ref[...] = jnp.full_like(m_ref, -jnp.inf)
        l_ref[...] = jnp.zeros_like(l_ref)
        acc_ref[...] = jnp.zeros_like(acc_ref)

    def scores(cb, kpe_t, cos, sin):
        kn = _dot_nt(wt_ref[...], cb)
        ssq = jnp.sum((kn * kn).reshape(H, NOPE_DIM, SUB_POS), axis=1)
        sn = _dot_nt(qabs_ref[0], cb)
        ssq = ssq + jnp.sum(kpe_t * kpe_t, axis=0, keepdims=True)
        inv = lax.rsqrt(ssq * (1.0 / QK_DIM) + EPS)
        x1 = kpe_t[:ROPE_DIM // 2] * g1_ref[...]
        x2 = kpe_t[ROPE_DIM // 2:] * g2_ref[...]
        kr = jnp.concatenate([x1 * cos - x2 * sin, x1 * sin + x2 * cos], axis=0)
        sr = _dot(qpe_ref[0], kr.astype(BF16))
        return (sn + sr) * jnp.concatenate([inv] * T, axis=0) * (QK_DIM ** -0.5)

    def softmax_update(s, cb):
        m_old = m_ref[...]
        m_new = jnp.maximum(m_old, jnp.max(s, axis=-1, keepdims=True))
        alpha = jnp.exp(m_old - m_new)
        pr = jnp.exp(s - m_new)
        l_ref[...] = alpha * l_ref[...] + jnp.sum(pr, axis=-1, keepdims=True)
        acc_ref[...] = alpha * acc_ref[...] + _dot(pr.astype(BF16), cb)
        m_ref[...] = m_new

    @pl.when(t < last)
    def _():
        cbs, ss = [], []
        for j in range(P // pps):
            pages = range(j * pps, (j + 1) * pps)
            cb = jnp.concatenate([ckv_pages[p][0].astype(BF16) for p in pages], axis=0)
            kpe_t = jnp.concatenate([kpe_pages[p][0] for p in pages], axis=1)
            cbs.append(cb)
            ss.append(scores(cb, kpe_t, cos_ref[:, j * SUB_POS:(j + 1) * SUB_POS],
                             sin_ref[:, j * SUB_POS:(j + 1) * SUB_POS]))
        softmax_update(jnp.concatenate(ss, axis=1), jnp.concatenate(cbs, axis=0))

    @pl.when(t == last)
    def _():
        cb = jnp.concatenate([cnew_ref[...], jnp.zeros((SUB_POS - NR, lora), F32)], axis=0).astype(BF16)
        kpe_t = jnp.concatenate([knew_ref[0], jnp.zeros((ROPE_DIM, SUB_POS - page), F32)], axis=1)
        q_tok = lax.broadcasted_iota(jnp.int32, (R, SUB_POS), 0) // H
        lane = lax.broadcasted_iota(jnp.int32, (R, SUB_POS), 1)
        valid = jnp.logical_and(lane // T == b % (NR // T), lane % T <= q_tok)
        valid = jnp.logical_and(valid, lane < NR)
        s = scores(cb, kpe_t, cos_ref[:, :SUB_POS], sin_ref[:, :SUB_POS])
        softmax_update(jnp.where(valid, s, -jnp.inf), cb)
        o_ref[0] = acc_ref[...] / l_ref[...]


def _decode_attention(qabs, qpe, w_uk_t, cache_ckv, cache_kpe_t, page_table, ckv_all, kpe_new_t,
                      cos_t, sin_t, g_k, *, Bd, T, H, Mp):
    P = PAGES_PER_STEP
    n_pages = page_table.shape[1]
    page = cache_ckv.shape[1]
    lora = cache_ckv.shape[2]
    assert n_pages % P == 0 and SUB_POS % page == 0 and (P * page) % SUB_POS == 0
    steps = n_pages // P + 1
    R = T * H
    NR = 8
    assert NR % T == 0 and Mp % NR == 0
    spb = NR // T
    pt = page_table.reshape(-1).astype(jnp.int32)
    half = ROPE_DIM // 2
    g1 = g_k[NOPE_DIM:NOPE_DIM + half].reshape(half, 1)
    g2 = g_k[NOPE_DIM + half:].reshape(half, 1)

    def page_spec(p, shape):
        def idx(b, t, pt):
            return (pt[b * n_pages + jnp.minimum(t, steps - 2) * P + p], 0, 0)
        return pl.BlockSpec((1,) + shape, idx)

    in_specs = [
        pl.BlockSpec((1, R, lora), lambda b, t, pt: (b, 0, 0)),
        pl.BlockSpec((1, R, ROPE_DIM), lambda b, t, pt: (b, 0, 0)),
        pl.BlockSpec(w_uk_t.shape, lambda b, t, pt: (0, 0)),
    ]
    in_specs += [page_spec(p, (page, lora)) for p in range(P)]
    in_specs += [page_spec(p, (ROPE_DIM, page)) for p in range(P)]
    in_specs += [
        pl.BlockSpec((NR, lora), lambda b, t, pt: (Mp // NR + b // spb, 0)),
        pl.BlockSpec((1, ROPE_DIM, page), lambda b, t, pt: (b // spb, 0, 0)),
        pl.BlockSpec((half, P * page), lambda b, t, pt: (0, t)),
        pl.BlockSpec((half, P * page), lambda b, t, pt: (0, t)),
        pl.BlockSpec((half, 1), lambda b, t, pt: (0, 0)),
        pl.BlockSpec((half, 1), lambda b, t, pt: (0, 0)),
    ]
    return pl.pallas_call(
        functools.partial(_decode_kernel, T=T, H=H),
        grid_spec=pltpu.PrefetchScalarGridSpec(
            num_scalar_prefetch=1,
            grid=(Bd, steps),
            in_specs=in_specs,
            out_specs=pl.BlockSpec((1, R, lora), lambda b, t, pt: (b, 0, 0)),
            scratch_shapes=[
                pltpu.VMEM((R, 1), F32),
                pltpu.VMEM((R, 1), F32),
                pltpu.VMEM((R, lora), F32),
            ],
        ),
        out_shape=jax.ShapeDtypeStruct((Bd, R, lora), F32),
        compiler_params=_params(("parallel", "arbitrary")),
        name="mla_decode_attention",
    )(pt, qabs, qpe, w_uk_t, *([cache_ckv] * P), *([cache_kpe_t] * P), ckv_all, kpe_new_t, cos_t, sin_t, g1, g2)


def _head_values_kernel(c_ref, w_ref, alias_ref, o_ref):
    del alias_ref
    o_ref[...] = _dot(c_ref[...].astype(BF16), w_ref[...])


def _head_values(ctx, w_uv, attn, *, H, Mp, Ms):
    lora = w_uv.shape[0]
    return pl.pallas_call(
        _head_values_kernel,
        grid=(H,),
        in_specs=[
            pl.BlockSpec((Ms, lora), lambda h: (0, h)),
            pl.BlockSpec((lora, V_DIM), lambda h: (0, h)),
            pl.BlockSpec(memory_space=pl.ANY),
        ],
        out_specs=pl.BlockSpec((Ms, V_DIM), lambda h: (Mp // Ms, h)),
        out_shape=jax.ShapeDtypeStruct(attn.shape, F32),
        input_output_aliases={2: 0},
        compiler_params=_params(("parallel",)),
        name="mla_decode_values",
    )(ctx, w_uv, attn)


def _rope_tables(pos, d):
    freq = ROPE_THETA ** (-jnp.arange(0, d, 2, dtype=F32) / d)
    ang = pos.astype(F32)[:, None] * freq[None, :]
    return jnp.cos(ang), jnp.sin(ang)


def kernel(x_prompt, x_sample, state_ret, cache_ckv, cache_kpe, page_table, ln_mix, ln_ffn, ret_w_in, ret_w_out,
           kv_src_norm, w_dkv, ckv_norm, w_ukv, k_norm, mla_w_q, q_norm, mla_w_o, moe_w_gr, moe_b_gr, moe_w_er,
           moe_b_er, moe_w1, moe_w3, moe_w2):
    B, T, D = x_prompt.shape
    Bd, Td, _ = x_sample.shape
    n_a = state_ret.shape[0]
    depth = ln_mix.shape[0]
    RH, dk, dv = state_ret.shape[2:]
    lora, MH = w_ukv.shape[0], w_ukv.shape[1]
    Mp, Ms = B * T, Bd * Td
    M = Mp + Ms
    past = page_table.shape[1] * cache_ckv.shape[1]
    tm = _tile(math.gcd(Mp, Ms), 512)

    pos_p = jnp.arange(T)
    pos_s = past + jnp.arange(Td)
    log_g = jnp.log1p(-jnp.exp2(-5.0 - jnp.arange(RH, dtype=F32)))
    y = jnp.concatenate([x_prompt.reshape(Mp, D), x_sample.reshape(Ms, D)], axis=0)

    ret_p, ret_s = [], []
    ckv = kpe = kpad = vals = None
    for i in range(depth):
        if i == n_a:
            ckv, kpe = _latent(y, kv_src_norm, w_dkv.astype(BF16), ckv_norm, tm=tm)
            pos_all = jnp.concatenate([jnp.tile(pos_p, B), jnp.tile(pos_s, Bd)])
            cos_m, sin_m = _rope_tables(pos_all, ROPE_DIM)
            w_uk = w_ukv[..., :NOPE_DIM].reshape(lora, MH * NOPE_DIM).astype(BF16)
            w_uv = w_ukv[..., NOPE_DIM:].reshape(lora, MH * V_DIM).astype(BF16)
            kpad, vals = _kv_prep(ckv, kpe, w_uk, w_uv, k_norm, cos_m, sin_m, rows=Mp, H=MH, tm=tm)
        if i < n_a:
            w_in = ret_w_in[i].astype(BF16)
            qkvg = _matmul(y, w_in, g=ln_mix[i], tm=tm, tn=_tile(w_in.shape[1], 1024), name="ret_in_proj")
            cos_p, sin_p = _rope_tables(pos_p, dk)
            gated, s_p = _retention_prompt(qkvg, log_g, cos_p, sin_p, B=B, T=T, H=RH, dk=dk, dv=dv, M=M)
            cos_s, sin_s = _rope_tables(jnp.tile(pos_s, 8 // Td), dk)
            gated, s_s = _retention_sample(qkvg, log_g, cos_s, sin_s, state_ret[i], gated,
                                           Bd=Bd, T=Td, H=RH, dk=dk, dv=dv, Mp=Mp)
            ret_p.append(s_p)
            ret_s.append(s_s)
            y = _matmul(gated, ret_w_out[i].astype(BF16), res=y, tm=tm, tn=_tile(D, 512), name="ret_out_proj")
        else:
            j = i - n_a
            wq = mla_w_q[j].reshape(D, MH, QK_DIM)
            wq = jnp.pad(wq, ((0, 0), (0, 0), (0, QK_PAD - QK_DIM))).reshape(D, MH * QK_PAD).astype(BF16)
            qraw = _matmul(y, wq, g=ln_mix[i], tm=tm, tn=_tile(MH * QK_PAD, 1024), name="mla_q_proj")
            qpad = _q_prep(qraw, q_norm[j], cos_m, sin_m, H=MH, tm=tm)
            attn = _flash_prompt(qpad, kpad, vals, B=B, T=T, H=MH, M=M, tq=_tile(T, 512))
            qabs = _absorb(qpad, w_uk, k_norm, H=MH, Mp=Mp, Ms=Ms)
            qabs = qabs.reshape(MH, Bd, Td, lora).transpose(1, 2, 0, 3).reshape(Bd, Td * MH, lora)
            qpe = qpad[:, Mp:, NOPE_DIM:QK_DIM].reshape(MH, Bd, Td, ROPE_DIM)
            qpe = qpe.transpose(1, 2, 0, 3).reshape(Bd, Td * MH, ROPE_DIM)
            span = PAGES_PER_STEP * cache_ckv.shape[1]
            n_cols = past + span
            col = jnp.arange(n_cols)
            pos_cols = jnp.where(col < past, col, past + (col - past) % Td)
            cos_c, sin_c = _rope_tables(pos_cols, ROPE_DIM)
            w_uk_t = w_uk.T
            page = cache_ckv.shape[1]
            spb = 8 // Td
            kpe_new_t = kpe[Mp:].reshape(Bd // spb, spb * Td, ROPE_DIM).swapaxes(1, 2)
            kpe_new_t = jnp.pad(kpe_new_t, ((0, 0), (0, 0), (0, page - spb * Td)))
            ctx = _decode_attention(qabs, qpe, w_uk_t, cache_ckv, cache_kpe.swapaxes(1, 2), page_table, ckv,
                                    kpe_new_t, cos_c.T, sin_c.T, k_norm, Bd=Bd, T=Td, H=MH, Mp=Mp)
            attn = _head_values(ctx.reshape(Ms, MH * lora), w_uv, attn, H=MH, Mp=Mp, Ms=Ms)
            y = _matmul(attn, mla_w_o[j].astype(BF16), res=y, tm=tm, tn=_tile(D, 512), name="mla_out_proj")
        y = _moe(y, ln_ffn[i], moe_w_gr[i], moe_b_gr[i], moe_w_er[i], moe_b_er[i],
                 moe_w1, moe_w3, moe_w2, layer=i, tm=tm)

    yp = y[:Mp].reshape(B, T, D)
    ys = y[Mp:].reshape(Bd, Td, D)
    return (yp, ys, jnp.stack(ret_p, axis=0), jnp.stack(ret_s, axis=0),
            ckv[:Mp].reshape(B, T, lora), kpe[:Mp].reshape(B, T, ROPE_DIM),
            ckv[Mp:].reshape(Bd, Td, lora), kpe[Mp:].reshape(Bd, Td, ROPE_DIM))
```

```python
import functools
import math

import jax
import jax.numpy as jnp
from jax import lax
from jax.experimental import pallas as pl
from jax.experimental.pallas import tpu as pltpu

F32 = jnp.float32
BF16 = jnp.bfloat16
EPS = 1e-6
ROPE_THETA = 10000.0
RET_CHUNK = 128
NOPE_DIM = 128
ROPE_DIM = 64
V_DIM = 128
QK_DIM = NOPE_DIM + ROPE_DIM
QK_PAD = 256
N_GROUPS = 4
EXP_PER_GROUP = 4
PAGES_PER_STEP = 16
SUB_POS = 512
MIB = 1024 * 1024


def _params(sem, vmem_mib=48):
    return pltpu.CompilerParams(dimension_semantics=sem, vmem_limit_bytes=vmem_mib * MIB)


def _tile(n, pref):
    t = math.gcd(n, pref)
    assert t % 8 == 0 or t == n, (n, pref)
    return t


def _dot(a, b):
    return jnp.dot(a, b, preferred_element_type=F32)


def _dot_nt(a, b):
    return lax.dot_general(a, b, (((1,), (1,)), ((), ())), preferred_element_type=F32)


def _rms_scale(x, n):
    return lax.rsqrt(jnp.sum(x * x, axis=-1, keepdims=True) * (1.0 / n) + EPS)


def _rope_half(x, cos, sin):
    half = x.shape[-1] // 2
    x1, x2 = x[:, :half], x[:, half:]
    return jnp.concatenate([x1 * cos - x2 * sin, x1 * sin + x2 * cos], axis=-1)


def _mm_kernel(*refs, has_norm, has_res, has_into):
    it = iter(refs)
    x_ref = next(it)
    g_ref = next(it) if has_norm else None
    w_ref = next(it)
    r_ref = next(it) if has_res else None
    if has_into:
        next(it)
    o_ref = next(it)
    xb_ref = next(it)

    @pl.when(pl.program_id(1) == 0)
    def _():
        x = x_ref[...].astype(F32)
        if has_norm:
            x = x * _rms_scale(x, x.shape[-1]) * g_ref[...]
        xb_ref[...] = x.astype(BF16)

    acc = _dot(xb_ref[...], w_ref[...])
    if has_res:
        acc = r_ref[...] + acc
    o_ref[...] = acc.astype(o_ref.dtype)


def _matmul(x, w, *, g=None, res=None, row0=0, rows=None, into=None, tm, tn, name):
    M, K = x.shape
    N = w.shape[1]
    rows = M - row0 if rows is None else rows
    assert row0 % tm == 0 and rows % tm == 0
    rb0 = row0 // tm
    in_specs = [pl.BlockSpec((tm, K), lambda i, j: (rb0 + i, 0))]
    args = [x]
    if g is not None:
        in_specs.append(pl.BlockSpec((1, K), lambda i, j: (0, 0)))
        args.append(g.reshape(1, K))
    in_specs.append(pl.BlockSpec((K, tn), lambda i, j: (0, j)))
    args.append(w)
    if res is not None:
        in_specs.append(pl.BlockSpec((tm, tn), lambda i, j: (rb0 + i, j)))
        args.append(res)
    aliases = {}
    if into is not None:
        aliases = {len(args): 0}
        in_specs.append(pl.BlockSpec(memory_space=pl.ANY))
        args.append(into)
    return pl.pallas_call(
        functools.partial(_mm_kernel, has_norm=g is not None, has_res=res is not None, has_into=into is not None),
        grid=(rows // tm, N // tn),
        in_specs=in_specs,
        out_specs=pl.BlockSpec((tm, tn), lambda i, j: (rb0 + i, j)),
        out_shape=jax.ShapeDtypeStruct((M, N), F32),
        input_output_aliases=aliases,
        scratch_shapes=[pltpu.VMEM((tm, K), BF16)],
        compiler_params=_params(("parallel", "arbitrary")),
        name=name,
    )(*args)


def _matmul_streams(x, w, *, Mp, tm_p, tm_s, **kw):
    name = kw.pop("name")
    out = _matmul(x, w, rows=Mp, tm=tm_p, name=name + "_prompt", **kw)
    return _matmul(x, w, row0=Mp, into=out, tm=tm_s, name=name + "_decode", **kw)


def _ret_prompt_kernel(lg_ref, q_ref, k_ref, v_ref, g_ref, cos_ref, sin_ref, o_ref, s_out_ref, s_ref):
    hg = pl.program_id(1)
    c = pl.program_id(2)
    L = q_ref.shape[0]
    HB, dk, dv = s_ref.shape

    @pl.when(c == 0)
    def _():
        s_ref[...] = jnp.zeros_like(s_ref)

    cos, sin = cos_ref[...], sin_ref[...]
    ii = lax.broadcasted_iota(jnp.int32, (L, L), 0)
    jj = lax.broadcasted_iota(jnp.int32, (L, L), 1)
    rel = (ii - jj).astype(F32)
    ri = lax.broadcasted_iota(jnp.int32, (L, 1), 0).astype(F32)
    for hh in range(HB):
        lg = lg_ref[hg * HB + hh]
        q = _rope_half(q_ref[:, hh * dk:(hh + 1) * dk], cos, sin)
        k = _rope_half(k_ref[:, hh * dk:(hh + 1) * dk], cos, sin) * (dk ** -0.5)
        vb = v_ref[:, hh * dv:(hh + 1) * dv].astype(BF16)
        qb = q.astype(BF16)
        decay = jnp.where(rel >= 0, jnp.exp(lg * jnp.maximum(rel, 0.0)), 0.0)
        scores = _dot_nt(qb, k.astype(BF16)) * decay
        inner = _dot(scores.astype(BF16), vb)
        s_old = s_ref[hh]
        cross = _dot(qb, s_old.astype(BF16)) * jnp.exp((ri + 1.0) * lg)
        k_dec = k * jnp.exp((L - 1.0 - ri) * lg)
        g_all = jnp.exp(jnp.zeros((1, 1), F32) + L * lg)
        s_new = g_all * s_old + _dot(k_dec.T.astype(BF16), vb)
        s_ref[hh] = s_new
        o = inner + cross
        o = o * _rms_scale(o, dv)
        gate = g_ref[:, hh * dv:(hh + 1) * dv]
        o_ref[:, hh * dv:(hh + 1) * dv] = (gate * jax.nn.sigmoid(gate)) * o

    @pl.when(c == pl.num_programs(2) - 1)
    def _():
        s_out_ref[0] = s_ref[...]


def _retention_prompt(qkvg, log_g, cos, sin, *, B, T, H, dk, dv, M):
    C = RET_CHUNK if T % RET_CHUNK == 0 else T
    nC = T // C
    HB = math.gcd(H, 4)
    assert (2 * H * dk) % (HB * dv) == 0
    nH = H // HB
    voff = 2 * H * dk // (HB * dv)
    return pl.pallas_call(
        _ret_prompt_kernel,
        grid=(B, nH, nC),
        in_specs=[
            pl.BlockSpec(memory_space=pltpu.SMEM),
            pl.BlockSpec((C, HB * dk), lambda b, h, c: (b * nC + c, h)),
            pl.BlockSpec((C, HB * dk), lambda b, h, c: (b * nC + c, nH + h)),
            pl.BlockSpec((C, HB * dv), lambda b, h, c: (b * nC + c, voff + h)),
            pl.BlockSpec((C, HB * dv), lambda b, h, c: (b * nC + c, voff + nH + h)),
            pl.BlockSpec((C, dk // 2), lambda b, h, c: (c, 0)),
            pl.BlockSpec((C, dk // 2), lambda b, h, c: (c, 0)),
        ],
        out_specs=[
            pl.BlockSpec((C, HB * dv), lambda b, h, c: (b * nC + c, h)),
            pl.BlockSpec((1, HB, dk, dv), lambda b, h, c: (b, h, 0, 0)),
        ],
        out_shape=[
            jax.ShapeDtypeStruct((M, H * dv), F32),
            jax.ShapeDtypeStruct((B, H, dk, dv), F32),
        ],
        scratch_shapes=[pltpu.VMEM((HB, dk, dv), F32)],
        compiler_params=_params(("parallel", "parallel", "arbitrary")),
        name="retention_prompt",
    )(log_g, qkvg, qkvg, qkvg, qkvg, cos, sin)


def _ret_sample_kernel(lg_ref, q_ref, k_ref, v_ref, g_ref, cos_ref, sin_ref, s_in_ref, o_alias_ref,
                       o_ref, s_out_ref, *, T):
    del o_alias_ref
    hg = pl.program_id(1)
    R = q_ref.shape[0]
    HB, dk, dv = s_in_ref.shape[1:]
    cos, sin = cos_ref[...], sin_ref[...]
    ii = lax.broadcasted_iota(jnp.int32, (R, R), 0)
    jj = lax.broadcasted_iota(jnp.int32, (R, R), 1)
    rel = (ii - jj).astype(F32)
    same = (ii // T) == (jj // T)
    row = lax.broadcasted_iota(jnp.int32, (R, 1), 0)
    tok = (row % T).astype(F32)
    pad = RET_CHUNK - R
    for hh in range(HB):
        lg = lg_ref[hg * HB + hh]
        q = _rope_half(q_ref[:, hh * dk:(hh + 1) * dk], cos, sin)
        k = _rope_half(k_ref[:, hh * dk:(hh + 1) * dk], cos, sin) * (dk ** -0.5)
        v = v_ref[:, hh * dv:(hh + 1) * dv]
        qb = q.astype(BF16)
        decay = jnp.where(same, jnp.where(rel >= 0, jnp.exp(lg * jnp.maximum(rel, 0.0)), 0.0), 0.0)
        scores = _dot_nt(qb, k.astype(BF16)) * decay
        sc = scores.astype(BF16).astype(F32)
        vr = v.astype(BF16).astype(F32)
        inner = jnp.zeros((R, dv), F32)
        for j in range(R):
            inner = inner + sc[:, j:j + 1] * vr[j:j + 1, :]
        k_dec = k * jnp.exp((T - 1.0 - tok) * lg)
        g_all = jnp.exp(jnp.zeros((1, 1), F32) + T * lg)
        v_pad = jnp.concatenate([v, jnp.zeros((pad, dv), F32)], axis=0).astype(BF16)
        cross = jnp.zeros((R, dv), F32)
        for s in range(R // T):
            mine = (row // T) == s
            s_old = s_in_ref[s, hh]
            cross = jnp.where(mine, _dot(qb, s_old.astype(BF16)), cross)
            kd = jnp.where(mine, k_dec, 0.0)
            kd_t = jnp.concatenate([kd, jnp.zeros((pad, dk), F32)], axis=0).T.astype(BF16)
            s_out_ref[s, hh] = g_all * s_old + _dot(kd_t, v_pad)
        o = inner + cross * jnp.exp((tok + 1.0) * lg)
        o = o * _rms_scale(o, dv)
        gate = g_ref[:, hh * dv:(hh + 1) * dv]
        o_ref[:, hh * dv:(hh + 1) * dv] = (gate * jax.nn.sigmoid(gate)) * o


def _retention_sample(qkvg, log_g, cos, sin, state, gated, *, Bd, T, H, dk, dv, Mp):
    R = 8
    assert R % T == 0 and Bd % (R // T) == 0 and Mp % R == 0
    spb = R // T
    rb0 = Mp // R
    HB = math.gcd(H, 4)
    assert (2 * H * dk) % (HB * dv) == 0
    nH = H // HB
    voff = 2 * H * dk // (HB * dv)
    M = qkvg.shape[0]
    return pl.pallas_call(
        functools.partial(_ret_sample_kernel, T=T),
        grid=(Bd // spb, nH),
        in_specs=[
            pl.BlockSpec(memory_space=pltpu.SMEM),
            pl.BlockSpec((R, HB * dk), lambda i, h: (rb0 + i, h)),
            pl.BlockSpec((R, HB * dk), lambda i, h: (rb0 + i, nH + h)),
            pl.BlockSpec((R, HB * dv), lambda i, h: (rb0 + i, voff + h)),
            pl.BlockSpec((R, HB * dv), lambda i, h: (rb0 + i, voff + nH + h)),
            pl.BlockSpec((R, dk // 2), lambda i, h: (0, 0)),
            pl.BlockSpec((R, dk // 2), lambda i, h: (0, 0)),
            pl.BlockSpec((spb, HB, dk, dv), lambda i, h: (i, h, 0, 0)),
            pl.BlockSpec(memory_space=pl.ANY),
        ],
        out_specs=[
            pl.BlockSpec((R, HB * dv), lambda i, h: (rb0 + i, h)),
            pl.BlockSpec((spb, HB, dk, dv), lambda i, h: (i, h, 0, 0)),
        ],
        out_shape=[
            jax.ShapeDtypeStruct((M, H * dv), F32),
            jax.ShapeDtypeStruct((Bd, H, dk, dv), F32),
        ],
        input_output_aliases={8: 0},
        compiler_params=_params(("parallel", "parallel")),
        name="retention_sample",
    )(log_g, qkvg, qkvg, qkvg, qkvg, cos, sin, state, gated)


def _router_kernel(y_ref, g_ref, wr_ref, br_ref, route_ref, cnt_ref, carry_ref):
    i = pl.program_id(0)
    tm = y_ref.shape[0]
    NE = N_GROUPS * EXP_PER_GROUP

    @pl.when(i == 0)
    def _():
        carry_ref[...] = jnp.zeros_like(carry_ref)

    y = y_ref[...]
    hn = y * _rms_scale(y, y.shape[-1]) * g_ref[...]
    lt = lax.dot_general(wr_ref[...], hn, (((1,), (1,)), ((), ())),
                         precision=lax.Precision.HIGHEST, preferred_element_type=F32) + br_ref[...]
    gl = [lt[k:k + 1, :] for k in range(N_GROUPS)]
    gmax = functools.reduce(jnp.maximum, gl)
    grp = jnp.full(gmax.shape, N_GROUPS - 1, jnp.int32)
    for k in range(N_GROUPS - 2, -1, -1):
        grp = jnp.where(gl[k] == gmax, k, grp)
    pg = 1.0 / functools.reduce(lambda a, b: a + b, [jnp.exp(x - gmax) for x in gl])
    el = []
    for k in range(EXP_PER_GROUP):
        sel = lt[N_GROUPS + (N_GROUPS - 1) * EXP_PER_GROUP + k:N_GROUPS + (N_GROUPS - 1) * EXP_PER_GROUP + k + 1, :]
        for gi in range(N_GROUPS - 2, -1, -1):
            r = N_GROUPS + gi * EXP_PER_GROUP + k
            sel = jnp.where(grp == gi, lt[r:r + 1, :], sel)
        el.append(sel)
    v1 = functools.reduce(jnp.maximum, el)
    i1 = jnp.full(v1.shape, EXP_PER_GROUP - 1, jnp.int32)
    for k in range(EXP_PER_GROUP - 2, -1, -1):
        i1 = jnp.where(el[k] == v1, k, i1)
    el2 = [jnp.where(i1 == k, -jnp.inf, el[k]) for k in range(EXP_PER_GROUP)]
    v2 = functools.reduce(jnp.maximum, el2)
    i2 = jnp.full(v2.shape, EXP_PER_GROUP - 1, jnp.int32)
    for k in range(EXP_PER_GROUP - 2, -1, -1):
        i2 = jnp.where(el2[k] == v2, k, i2)
    e21 = jnp.exp(v2 - v1)
    den = 1.0 / (1.0 + e21)
    c1 = pg * den
    c2 = pg * (e21 * den)
    e1 = grp * EXP_PER_GROUP + i1
    e2 = grp * EXP_PER_GROUP + i2
    eid = lax.broadcasted_iota(jnp.int32, (NE, tm), 0)
    oh1 = (eid == e1).astype(F32)
    oh2 = (eid == e2).astype(F32)
    oh = oh1 + oh2
    tr = lax.broadcasted_iota(jnp.int32, (tm, tm), 0)
    tc = lax.broadcasted_iota(jnp.int32, (tm, tm), 1)
    before = jnp.where(tr < tc, 1.0, 0.0).astype(BF16)
    base = carry_ref[...] + _dot(oh.astype(BF16), before)
    r1 = jnp.sum(oh1 * base, axis=0, keepdims=True)
    r2 = jnp.sum(oh2 * base, axis=0, keepdims=True)
    total = carry_ref[...] + jnp.sum(oh, axis=1, keepdims=True)
    carry_ref[...] = total
    cnt_ref[...] = jnp.broadcast_to(total, cnt_ref.shape)
    route_ref[...] = jnp.concatenate(
        [e1.astype(F32), e2.astype(F32), c1, c2, r1, r2, jnp.zeros((2, tm), F32)], axis=0)


def _router(y, g, w_gr, b_gr, w_er, b_er, *, tm):
    M, D = y.shape
    NE = N_GROUPS * EXP_PER_GROUP
    rows = 32
    wr = jnp.zeros((rows, D), F32).at[:N_GROUPS].set(w_gr.T).at[N_GROUPS:N_GROUPS + NE].set(w_er.T)
    br = jnp.zeros((rows, 1), F32).at[:N_GROUPS, 0].set(b_gr).at[N_GROUPS:N_GROUPS + NE, 0].set(b_er)
    return pl.pallas_call(
        _router_kernel,
        grid=(M // tm,),
        in_specs=[
            pl.BlockSpec((tm, D), lambda i: (i, 0)),
            pl.BlockSpec((1, D), lambda i: (0, 0)),
            pl.BlockSpec((rows, D), lambda i: (0, 0)),
            pl.BlockSpec((rows, 1), lambda i: (0, 0)),
        ],
        out_specs=[
            pl.BlockSpec((8, tm), lambda i: (0, i)),
            pl.BlockSpec((NE, 128), lambda i: (0, 0)),
        ],
        out_shape=[
            jax.ShapeDtypeStruct((8, M), F32),
            jax.ShapeDtypeStruct((NE, 128), F32),
        ],
        scratch_shapes=[pltpu.VMEM((NE, 1), F32)],
        compiler_params=_params(("arbitrary",)),
        name="moe_router",
    )(y, g.reshape(1, D), wr, br)


def _ffn_kernel(te_ref, nu_ref, x_ref, g_ref, w1_ref, w3_ref, w2_ref, o_ref, xb_ref):
    del te_ref
    i = pl.program_id(0)
    f = pl.program_id(1)
    used = i < nu_ref[0]

    @pl.when(jnp.logical_and(used, f == 0))
    def _():
        x = x_ref[...]
        xb_ref[...] = (x * _rms_scale(x, x.shape[-1]) * g_ref[...]).astype(BF16)

    @pl.when(used)
    def _():
        x = xb_ref[...]
        a = _dot(x, w1_ref[0])
        b = _dot(x, w3_ref[0])
        hh = (a * jax.nn.sigmoid(a)) * b
        y = _dot(hh.astype(BF16), w2_ref[0])

        @pl.when(f == 0)
        def _():
            o_ref[...] = y

        @pl.when(f > 0)
        def _():
            o_ref[...] += y

    @pl.when(jnp.logical_and(jnp.logical_not(used), f == 0))
    def _():
        o_ref[...] = jnp.zeros_like(o_ref)


def _expert_ffn(xs, g, tile_expert, n_used, w1, w3, w2, *, tm, tf):
    NS, D = xs.shape
    FF = w1.shape[2]
    nT = NS // tm

    def xi(i, f, te, nu):
        return (jnp.minimum(i, nu[0] - 1), 0)

    return pl.pallas_call(
        _ffn_kernel,
        grid_spec=pltpu.PrefetchScalarGridSpec(
            num_scalar_prefetch=2,
            grid=(nT, FF // tf),
            in_specs=[
                pl.BlockSpec((tm, D), xi),
                pl.BlockSpec((1, D), lambda i, f, te, nu: (0, 0)),
                pl.BlockSpec((1, D, tf), lambda i, f, te, nu: (te[i], 0, f)),
                pl.BlockSpec((1, D, tf), lambda i, f, te, nu: (te[i], 0, f)),
                pl.BlockSpec((1, tf, D), lambda i, f, te, nu: (te[i], f, 0)),
            ],
            out_specs=pl.BlockSpec((tm, D), lambda i, f, te, nu: (i, 0)),
            scratch_shapes=[pltpu.VMEM((tm, D), BF16)],
        ),
        out_shape=jax.ShapeDtypeStruct((NS, D), F32),
        compiler_params=_params(("arbitrary", "arbitrary")),
        name="moe_ffn",
    )(tile_expert, n_used, xs, g.reshape(1, D), w1, w3, w2)


def _combine_kernel(y_ref, a_ref, b_ref, c_ref, o_ref):
    c = c_ref[...]
    o_ref[...] = y_ref[...] + (c[:, 0:1] * a_ref[...] + c[:, 1:2] * b_ref[...])


def _combine(y, a, b, c, *, tm):
    M, D = y.shape
    row = pl.BlockSpec((tm, D), lambda i: (i, 0))
    return pl.pallas_call(
        _combine_kernel,
        grid=(M // tm,),
        in_specs=[row, row, row, pl.BlockSpec((tm, 2), lambda i: (i, 0))],
        out_specs=row,
        out_shape=jax.ShapeDtypeStruct((M, D), F32),
        compiler_params=_params(("parallel",)),
        name="moe_combine",
    )(y, a, b, c)


def _moe(y, g, w_gr, b_gr, w_er, b_er, w1, w3, w2, *, layer, tm):
    M, D = y.shape
    NE = N_GROUPS * EXP_PER_GROUP
    route, cnt = _router(y, g, w_gr, b_gr, w_er, b_er, tm=tm)
    e1 = route[0].astype(jnp.int32)
    e2 = route[1].astype(jnp.int32)
    r1 = route[4].astype(jnp.int32)
    r2 = route[5].astype(jnp.int32)
    counts = cnt[:, 0].astype(jnp.int32)
    padded = ((counts + tm - 1) // tm) * tm
    ends = jnp.cumsum(padded)
    off = ends - padded
    d1 = off[e1] + r1
    d2 = off[e2] + r2
    NS = ((2 * M + NE * (tm - 1)) // tm) * tm
    tok = jnp.arange(M, dtype=jnp.int32)
    src = (jnp.arange(NS, dtype=jnp.int32) % M).at[jnp.concatenate([d1, d2])].set(
        jnp.concatenate([tok, tok]), unique_indices=True, mode="promise_in_bounds")
    n_used = (ends[-1] // tm).astype(jnp.int32).reshape(1)
    tile_start = jnp.minimum(jnp.arange(NS // tm, dtype=jnp.int32) * tm, ends[-1] - tm)
    tile_expert = jnp.sum((ends[None, :] <= tile_start[:, None]).astype(jnp.int32), axis=1)
    tile_expert = jnp.minimum(tile_expert, NE - 1)
    xs = y.at[src].get(mode="promise_in_bounds")
    w1b, w3b, w2b = (w[layer].astype(BF16) for w in (w1, w3, w2))
    tf = _tile(w1.shape[3], 512)
    ys = _expert_ffn(xs, g, tile_expert, n_used, w1b, w3b, w2b, tm=tm, tf=tf)
    a = ys.at[d1].get(mode="promise_in_bounds")
    b = ys.at[d2].get(mode="promise_in_bounds")
    return _combine(y, a, b, route[2:4].T, tm=tm)


def _latent_kernel(x_ref, gs_ref, w_ref, gc_ref, ckv_ref, kpe_ref):
    x = x_ref[...]
    s = x * _rms_scale(x, x.shape[-1]) * gs_ref[...]
    ck = _dot(s.astype(BF16), w_ref[...])
    lora = ckv_ref.shape[1]
    c = ck[:, :lora]
    ckv_ref[...] = c * _rms_scale(c, lora) * gc_ref[...]
    kpe_ref[...] = ck[:, lora:]


def _latent(y, g_src, w_dkv, g_ckv, *, tm):
    M, D = y.shape
    lora = g_ckv.shape[0]
    N = w_dkv.shape[1]
    return pl.pallas_call(
        _latent_kernel,
        grid=(M // tm,),
        in_specs=[
            pl.BlockSpec((tm, D), lambda i: (i, 0)),
            pl.BlockSpec((1, D), lambda i: (0, 0)),
            pl.BlockSpec((D, N), lambda i: (0, 0)),
            pl.BlockSpec((1, lora), lambda i: (0, 0)),
        ],
        out_specs=[
            pl.BlockSpec((tm, lora), lambda i: (i, 0)),
            pl.BlockSpec((tm, N - lora), lambda i: (i, 0)),
        ],
        out_shape=[
            jax.ShapeDtypeStruct((M, lora), F32),
            jax.ShapeDtypeStruct((M, N - lora), F32),
        ],
        compiler_params=_params(("parallel",)),
        name="shared_latent",
    )(y, g_src.reshape(1, D), w_dkv, g_ckv.reshape(1, lora))


def _head_norm_rope(nope, pe, ssq, gain, cos, sin):
    inv = lax.rsqrt(ssq * (1.0 / QK_DIM) + EPS)
    n = nope * inv * gain[:, :NOPE_DIM]
    p = _rope_half(pe * inv * gain[:, NOPE_DIM:], cos, sin)
    return jnp.concatenate([n, p, jnp.zeros((n.shape[0], QK_PAD - QK_DIM), F32)], axis=-1)


def _kv_prep_kernel(ckv_ref, kpe_ref, wuk_ref, wuv_ref, gk_ref, cos_ref, sin_ref, k_ref, v_ref):
    cb = ckv_ref[...].astype(BF16)
    kn = _dot(cb, wuk_ref[...])
    v_ref[...] = _dot(cb, wuv_ref[...]).astype(v_ref.dtype)
    pe = kpe_ref[...]
    ssq = jnp.sum(kn * kn, axis=-1, keepdims=True) + jnp.sum(pe * pe, axis=-1, keepdims=True)
    k_ref[0] = _head_norm_rope(kn, pe, ssq, gk_ref[...], cos_ref[...], sin_ref[...]).astype(k_ref.dtype)


def _kv_prep(ckv, kpe, w_uk, w_uv, g_k, cos, sin, *, rows, H, tm):
    lora = ckv.shape[1]
    return pl.pallas_call(
        _kv_prep_kernel,
        grid=(rows // tm, H),
        in_specs=[
            pl.BlockSpec((tm, lora), lambda i, h: (i, 0)),
            pl.BlockSpec((tm, ROPE_DIM), lambda i, h: (i, 0)),
            pl.BlockSpec((lora, NOPE_DIM), lambda i, h: (0, h)),
            pl.BlockSpec((lora, V_DIM), lambda i, h: (0, h)),
            pl.BlockSpec((1, QK_DIM), lambda i, h: (0, 0)),
            pl.BlockSpec((tm, ROPE_DIM // 2), lambda i, h: (i, 0)),
            pl.BlockSpec((tm, ROPE_DIM // 2), lambda i, h: (i, 0)),
        ],
        out_specs=[
            pl.BlockSpec((1, tm, QK_PAD), lambda i, h: (h, i, 0)),
            pl.BlockSpec((tm, V_DIM), lambda i, h: (i, h)),
        ],
        out_shape=[
            jax.ShapeDtypeStruct((H, rows, QK_PAD), BF16),
            jax.ShapeDtypeStruct((rows, H * V_DIM), BF16),
        ],
        compiler_params=_params(("parallel", "arbitrary")),
        name="mla_kv_prep",
    )(ckv, kpe, w_uk, w_uv, g_k.reshape(1, QK_DIM), cos, sin)


def _q_prep_kernel(q_ref, gq_ref, cos_ref, sin_ref, o_ref):
    q = q_ref[...]
    ssq = jnp.sum(q * q, axis=-1, keepdims=True)
    o_ref[0] = _head_norm_rope(q[:, :NOPE_DIM], q[:, NOPE_DIM:QK_DIM], ssq, gq_ref[...],
                               cos_ref[...], sin_ref[...]).astype(o_ref.dtype)


def _q_prep(qraw, g_q, cos, sin, *, H, tm):
    M = qraw.shape[0]
    return pl.pallas_call(
        _q_prep_kernel,
        grid=(M // tm, H),
        in_specs=[
            pl.BlockSpec((tm, QK_PAD), lambda i, h: (i, h)),
            pl.BlockSpec((1, QK_DIM), lambda i, h: (0, 0)),
            pl.BlockSpec((tm, ROPE_DIM // 2), lambda i, h: (i, 0)),
            pl.BlockSpec((tm, ROPE_DIM // 2), lambda i, h: (i, 0)),
        ],
        out_specs=pl.BlockSpec((1, tm, QK_PAD), lambda i, h: (h, i, 0)),
        out_shape=jax.ShapeDtypeStruct((H, M, QK_PAD), BF16),
        compiler_params=_params(("parallel", "arbitrary")),
        name="mla_q_prep",
    )(qraw, g_q.reshape(1, QK_DIM), cos, sin)


def _flash_kernel(q_ref, k_ref, v_ref, o_ref, m_ref, acc_ref):
    qi = pl.program_id(2)
    HB, tq, _ = q_ref.shape
    scale = QK_DIM ** -0.5
    m_ref[...] = jnp.full_like(m_ref, -jnp.inf)
    acc_ref[...] = jnp.zeros_like(acc_ref)
    ones = jnp.ones((tq, V_DIM), BF16)

    def step(start, masked):
        for h in range(HB):
            k = k_ref[h, pl.ds(start, tq), :]
            v = jnp.concatenate([v_ref[pl.ds(start, tq), h * V_DIM:(h + 1) * V_DIM], ones], axis=1)
            s = _dot_nt(q_ref[h], k) * scale
            if masked:
                row = lax.broadcasted_iota(jnp.int32, (tq, tq), 0)
                col = lax.broadcasted_iota(jnp.int32, (tq, tq), 1)
                s = jnp.where(col <= row, s, -jnp.inf)
            m_old = m_ref[h]
            m_new = jnp.maximum(m_old, jnp.max(s, axis=-1, keepdims=True))
            p = jnp.exp(s - m_new)
            acc_ref[h] = jnp.exp(m_old - m_new) * acc_ref[h] + _dot(p.astype(BF16), v)
            m_ref[h] = m_new

    def full_tile(ki, carry):
        step(pl.multiple_of(ki * tq, tq), False)
        return carry

    lax.fori_loop(0, qi, full_tile, 0)
    step(pl.multiple_of(qi * tq, tq), True)
    for h in range(HB):
        acc = acc_ref[h]
        o_ref[:, h * V_DIM:(h + 1) * V_DIM] = acc[:, :V_DIM] / acc[:, V_DIM:V_DIM + 1]


def _flash_prompt(qpad, kpad, v, *, B, T, H, M, tq):
    nq = T // tq
    HB = math.gcd(H, 2)
    return pl.pallas_call(
        _flash_kernel,
        grid=(B, H // HB, nq),
        in_specs=[
            pl.BlockSpec((HB, tq, QK_PAD), lambda b, h, qi: (h, b * nq + qi, 0)),
            pl.BlockSpec((HB, T, QK_PAD), lambda b, h, qi: (h, b, 0)),
            pl.BlockSpec((T, HB * V_DIM), lambda b, h, qi: (b, h)),
        ],
        out_specs=pl.BlockSpec((tq, HB * V_DIM), lambda b, h, qi: (b * nq + qi, h)),
        out_shape=jax.ShapeDtypeStruct((M, H * V_DIM), F32),
        scratch_shapes=[pltpu.VMEM((HB, tq, 1), F32), pltpu.VMEM((HB, tq, 2 * V_DIM), F32)],
        compiler_params=_params(("parallel", "parallel", "arbitrary")),
        name="mla_prompt_attention",
    )(qpad, kpad, v)


def _absorb_kernel(q_ref, wuk_ref, gk_ref, o_ref):
    qg = q_ref[0][:, :NOPE_DIM].astype(F32) * gk_ref[...]
    o_ref[0] = _dot_nt(qg.astype(BF16), wuk_ref[...]).astype(o_ref.dtype)


def _absorb(qpad, w_uk, g_k, *, H, Mp, Ms):
    lora = w_uk.shape[0]
    assert Mp % Ms == 0
    return pl.pallas_call(
        _absorb_kernel,
        grid=(H,),
        in_specs=[
            pl.BlockSpec((1, Ms, QK_PAD), lambda h: (h, Mp // Ms, 0)),
            pl.BlockSpec((lora, NOPE_DIM), lambda h: (0, h)),
            pl.BlockSpec((1, NOPE_DIM), lambda h: (0, 0)),
        ],
        out_specs=pl.BlockSpec((1, Ms, lora), lambda h: (h, 0, 0)),
        out_shape=jax.ShapeDtypeStruct((H, Ms, lora), BF16),
        compiler_params=_params(("parallel",)),
        name="mla_absorb_queries",
    )(qpad, w_uk, g_k[:NOPE_DIM].reshape(1, NOPE_DIM))


def _decode_kernel(pt_ref, qabs_ref, qpe_ref, wt_ref, *refs, T, H):
    del pt_ref
    P = PAGES_PER_STEP
    ckv_pages = refs[:P]
    kpe_pages = refs[P:2 * P]
    (cnew_ref, knew_ref, cos_ref, sin_ref, cosn_ref, sinn_ref, g1_ref, g2_ref,
     o_ref, m_ref, l_ref, acc_ref) = refs[2 * P:]
    b = pl.program_id(0)
    t = pl.program_id(1)
    last = pl.num_programs(1) - 1
    page = ckv_pages[0].shape[1]
    lora = ckv_pages[0].shape[2]
    R = T * H
    NR = cnew_ref.shape[0]
    pps = SUB_POS // page

    @pl.when(t == 0)
    def _():
        m_ref[...] = jnp.full_like(m_ref, -jnp.inf)
        l_ref[...] = jnp.zeros_like(l_ref)
        acc_ref[...] = jnp.zeros_like(acc_ref)

    def scores(cb, kpe_t, cos, sin):
        W = cb.shape[0]
        kn = _dot_nt(wt_ref[...], cb)
        ssq = jnp.sum((kn * kn).reshape(H, NOPE_DIM, W), axis=1)
        sn = _dot_nt(qabs_ref[0], cb)
        ssq = ssq + jnp.sum(kpe_t * kpe_t, axis=0, keepdims=True)
        inv = lax.rsqrt(ssq * (1.0 / QK_DIM) + EPS)
        x1 = kpe_t[:ROPE_DIM // 2] * g1_ref[...]
        x2 = kpe_t[ROPE_DIM // 2:] * g2_ref[...]
        kr = jnp.concatenate([x1 * cos - x2 * sin, x1 * sin + x2 * cos], axis=0)
        sr = _dot(qpe_ref[0], kr.astype(BF16))
        return (sn + sr) * jnp.concatenate([inv] * T, axis=0) * (QK_DIM ** -0.5)

    def softmax_update(s, cb):
        m_old = m_ref[...]
        m_new = jnp.maximum(m_old, jnp.max(s, axis=-1, keepdims=True))
        alpha = jnp.exp(m_old - m_new)
        pr = jnp.exp(s - m_new)
        l_ref[...] = alpha * l_ref[...] + jnp.sum(pr, axis=-1, keepdims=True)
        acc_ref[...] = alpha * acc_ref[...] + _dot(pr.astype(BF16), cb)
        m_ref[...] = m_new

    cbs, ss = [], []
    for j in range(P // pps):
        pages = range(j * pps, (j + 1) * pps)
        cb = jnp.concatenate([ckv_pages[p][0].astype(BF16) for p in pages], axis=0)
        kpe_t = jnp.concatenate([kpe_pages[p][0] for p in pages], axis=1)
        cbs.append(cb)
        ss.append(scores(cb, kpe_t, cos_ref[:, j * SUB_POS:(j + 1) * SUB_POS],
                         sin_ref[:, j * SUB_POS:(j + 1) * SUB_POS]))
    softmax_update(jnp.concatenate(ss, axis=1), jnp.concatenate(cbs, axis=0))

    @pl.when(t == last)
    def _():
        cb = jnp.concatenate([cnew_ref[...], jnp.zeros((page - NR, lora), F32)], axis=0).astype(BF16)
        q_tok = lax.broadcasted_iota(jnp.int32, (R, page), 0) // H
        lane = lax.broadcasted_iota(jnp.int32, (R, page), 1)
        valid = jnp.logical_and(lane // T == b % (NR // T), lane % T <= q_tok)
        valid = jnp.logical_and(valid, lane < NR)
        s = scores(cb, knew_ref[0], cosn_ref[...], sinn_ref[...])
        softmax_update(jnp.where(valid, s, -jnp.inf), cb)
        o_ref[0] = acc_ref[...] / l_ref[...]


def _decode_attention(qabs, qpe, w_uk_t, cache_ckv, cache_kpe_t, page_table, ckv_all, kpe_new_t,
                      cos_t, sin_t, g_k, *, Bd, T, H, Mp):
    P = PAGES_PER_STEP
    n_pages = page_table.shape[1]
    page = cache_ckv.shape[1]
    lora = cache_ckv.shape[2]
    assert n_pages % P == 0 and SUB_POS % page == 0 and (P * page) % SUB_POS == 0
    steps = n_pages // P
    R = T * H
    NR = 8
    assert NR % T == 0 and Mp % NR == 0
    spb = NR // T
    pt = page_table.reshape(-1).astype(jnp.int32)
    half = ROPE_DIM // 2
    g1 = g_k[NOPE_DIM:NOPE_DIM + half].reshape(half, 1)
    g2 = g_k[NOPE_DIM + half:].reshape(half, 1)

    def page_spec(p, shape):
        def idx(b, t, pt):
            return (pt[b * n_pages + t * P + p], 0, 0)
        return pl.BlockSpec((1,) + shape, idx)

    in_specs = [
        pl.BlockSpec((1, R, lora), lambda b, t, pt: (b, 0, 0)),
        pl.BlockSpec((1, R, ROPE_DIM), lambda b, t, pt: (b, 0, 0)),
        pl.BlockSpec(w_uk_t.shape, lambda b, t, pt: (0, 0)),
    ]
    in_specs += [page_spec(p, (page, lora)) for p in range(P)]
    in_specs += [page_spec(p, (ROPE_DIM, page)) for p in range(P)]
    in_specs += [
        pl.BlockSpec((NR, lora), lambda b, t, pt: (Mp // NR + b // spb, 0)),
        pl.BlockSpec((1, ROPE_DIM, page), lambda b, t, pt: (b // spb, 0, 0)),
        pl.BlockSpec((half, P * page), lambda b, t, pt: (0, t)),
        pl.BlockSpec((half, P * page), lambda b, t, pt: (0, t)),
        pl.BlockSpec((half, page), lambda b, t, pt: (0, n_pages)),
        pl.BlockSpec((half, page), lambda b, t, pt: (0, n_pages)),
        pl.BlockSpec((half, 1), lambda b, t, pt: (0, 0)),
        pl.BlockSpec((half, 1), lambda b, t, pt: (0, 0)),
    ]
    return pl.pallas_call(
        functools.partial(_decode_kernel, T=T, H=H),
        grid_spec=pltpu.PrefetchScalarGridSpec(
            num_scalar_prefetch=1,
            grid=(Bd, steps),
            in_specs=in_specs,
            out_specs=pl.BlockSpec((1, R, lora), lambda b, t, pt: (b, 0, 0)),
            scratch_shapes=[
                pltpu.VMEM((R, 1), F32),
                pltpu.VMEM((R, 1), F32),
                pltpu.VMEM((R, lora), F32),
            ],
        ),
        out_shape=jax.ShapeDtypeStruct((Bd, R, lora), F32),
        compiler_params=_params(("parallel", "arbitrary")),
        name="mla_decode_attention",
    )(pt, qabs, qpe, w_uk_t, *([cache_ckv] * P), *([cache_kpe_t] * P), ckv_all, kpe_new_t, cos_t, sin_t, cos_t, sin_t,
      g1, g2)


def _head_values_kernel(c_ref, w_ref, alias_ref, o_ref):
    del alias_ref
    o_ref[...] = _dot(c_ref[...].astype(BF16), w_ref[...])


def _head_values(ctx, w_uv, attn, *, H, Mp, Ms):
    lora = w_uv.shape[0]
    return pl.pallas_call(
        _head_values_kernel,
        grid=(H,),
        in_specs=[
            pl.BlockSpec((Ms, lora), lambda h: (0, h)),
            pl.BlockSpec((lora, V_DIM), lambda h: (0, h)),
            pl.BlockSpec(memory_space=pl.ANY),
        ],
        out_specs=pl.BlockSpec((Ms, V_DIM), lambda h: (Mp // Ms, h)),
        out_shape=jax.ShapeDtypeStruct(attn.shape, F32),
        input_output_aliases={2: 0},
        compiler_params=_params(("parallel",)),
        name="mla_decode_values",
    )(ctx, w_uv, attn)


def _rope_tables(pos, d):
    freq = ROPE_THETA ** (-jnp.arange(0, d, 2, dtype=F32) / d)
    ang = pos.astype(F32)[:, None] * freq[None, :]
    return jnp.cos(ang), jnp.sin(ang)


def kernel(x_prompt, x_sample, state_ret, cache_ckv, cache_kpe, page_table, ln_mix, ln_ffn, ret_w_in, ret_w_out,
           kv_src_norm, w_dkv, ckv_norm, w_ukv, k_norm, mla_w_q, q_norm, mla_w_o, moe_w_gr, moe_b_gr, moe_w_er,
           moe_b_er, moe_w1, moe_w3, moe_w2):
    B, T, D = x_prompt.shape
    Bd, Td, _ = x_sample.shape
    n_a = state_ret.shape[0]
    depth = ln_mix.shape[0]
    RH, dk, dv = state_ret.shape[2:]
    lora, MH = w_ukv.shape[0], w_ukv.shape[1]
    Mp, Ms = B * T, Bd * Td
    M = Mp + Ms
    past = page_table.shape[1] * cache_ckv.shape[1]
    tm = _tile(math.gcd(Mp, Ms), 512)
    tm_p = _tile(Mp, 1024)

    pos_p = jnp.arange(T)
    pos_s = past + jnp.arange(Td)
    log_g = jnp.log1p(-jnp.exp2(-5.0 - jnp.arange(RH, dtype=F32)))
    y = jnp.concatenate([x_prompt.reshape(Mp, D), x_sample.reshape(Ms, D)], axis=0)

    ret_p, ret_s = [], []
    ckv = kpe = kpad = vals = None
    for i in range(depth):
        if i == n_a:
            ckv, kpe = _latent(y, kv_src_norm, w_dkv.astype(BF16), ckv_norm, tm=tm)
            pos_all = jnp.concatenate([jnp.tile(pos_p, B), jnp.tile(pos_s, Bd)])
            cos_m, sin_m = _rope_tables(pos_all, ROPE_DIM)
            w_uk = w_ukv[..., :NOPE_DIM].reshape(lora, MH * NOPE_DIM).astype(BF16)
            w_uv = w_ukv[..., NOPE_DIM:].reshape(lora, MH * V_DIM).astype(BF16)
            kpad, vals = _kv_prep(ckv, kpe, w_uk, w_uv, k_norm, cos_m, sin_m, rows=Mp, H=MH, tm=tm)
        if i < n_a:
            w_in = ret_w_in[i].astype(BF16)
            qkvg = _matmul_streams(y, w_in, g=ln_mix[i], Mp=Mp, tm_p=tm_p, tm_s=tm,
                                   tn=_tile(w_in.shape[1], 1024), name="ret_in_proj")
            cos_p, sin_p = _rope_tables(pos_p, dk)
            gated, s_p = _retention_prompt(qkvg, log_g, cos_p, sin_p, B=B, T=T, H=RH, dk=dk, dv=dv, M=M)
            cos_s, sin_s = _rope_tables(jnp.tile(pos_s, 8 // Td), dk)
            gated, s_s = _retention_sample(qkvg, log_g, cos_s, sin_s, state_ret[i], gated,
                                           Bd=Bd, T=Td, H=RH, dk=dk, dv=dv, Mp=Mp)
            ret_p.append(s_p)
            ret_s.append(s_s)
            y = _matmul(gated, ret_w_out[i].astype(BF16), res=y, tm=tm, tn=_tile(D, 512), name="ret_out_proj")
        else:
            j = i - n_a
            wq = mla_w_q[j].reshape(D, MH, QK_DIM)
            wq = jnp.pad(wq, ((0, 0), (0, 0), (0, QK_PAD - QK_DIM))).reshape(D, MH * QK_PAD).astype(BF16)
            qraw = _matmul_streams(y, wq, g=ln_mix[i], Mp=Mp, tm_p=tm_p, tm_s=tm,
                                   tn=_tile(MH * QK_PAD, 1024), name="mla_q_proj")
            qpad = _q_prep(qraw, q_norm[j], cos_m, sin_m, H=MH, tm=tm)
            attn = _flash_prompt(qpad, kpad, vals, B=B, T=T, H=MH, M=M, tq=_tile(T, 512))
            qabs = _absorb(qpad, w_uk, k_norm, H=MH, Mp=Mp, Ms=Ms)
            qabs = qabs.reshape(MH, Bd, Td, lora).transpose(1, 2, 0, 3).reshape(Bd, Td * MH, lora)
            qpe = qpad[:, Mp:, NOPE_DIM:QK_DIM].reshape(MH, Bd, Td, ROPE_DIM)
            qpe = qpe.transpose(1, 2, 0, 3).reshape(Bd, Td * MH, ROPE_DIM)
            page = cache_ckv.shape[1]
            col = jnp.arange(past + page)
            pos_cols = jnp.where(col < past, col, past + (col - past) % Td)
            cos_c, sin_c = _rope_tables(pos_cols, ROPE_DIM)
            w_uk_t = w_uk.T
            spb = 8 // Td
            kpe_new_t = kpe[Mp:].reshape(Bd // spb, spb * Td, ROPE_DIM).swapaxes(1, 2)
            kpe_new_t = jnp.pad(kpe_new_t, ((0, 0), (0, 0), (0, page - spb * Td)))
            ctx = _decode_attention(qabs, qpe, w_uk_t, cache_ckv, cache_kpe.swapaxes(1, 2), page_table, ckv,
                                    kpe_new_t, cos_c.T, sin_c.T, k_norm, Bd=Bd, T=Td, H=MH, Mp=Mp)
            attn = _head_values(ctx.reshape(Ms, MH * lora), w_uv, attn, H=MH, Mp=Mp, Ms=Ms)
            y = _matmul_streams(attn, mla_w_o[j].astype(BF16), res=y, Mp=Mp, tm_p=tm_p, tm_s=tm,
                                tn=_tile(D, 512), name="mla_out_proj")
        y = _moe(y, ln_ffn[i], moe_w_gr[i], moe_b_gr[i], moe_w_er[i], moe_b_er[i],
                 moe_w1, moe_w3, moe_w2, layer=i, tm=tm)

    yp = y[:Mp].reshape(B, T, D)
    ys = y[Mp:].reshape(Bd, Td, D)
    return (yp, ys, jnp.stack(ret_p, axis=0), jnp.stack(ret_s, axis=0),
            ckv[:Mp].reshape(B, T, lora), kpe[:Mp].reshape(B, T, ROPE_DIM),
            ckv[Mp:].reshape(Bd, Td, lora), kpe[Mp:].reshape(Bd, Td, ROPE_DIM))
```

```python
import functools
import math

import jax
import jax.numpy as jnp
from jax import lax
from jax.experimental import pallas as pl
from jax.experimental.pallas import tpu as pltpu

F32 = jnp.float32
BF16 = jnp.bfloat16
EPS = 1e-6
ROPE_THETA = 10000.0
RET_CHUNK = 128
NOPE_DIM = 128
ROPE_DIM = 64
V_DIM = 128
QK_DIM = NOPE_DIM + ROPE_DIM
QK_PAD = 256
N_GROUPS = 4
EXP_PER_GROUP = 4
PAGES_PER_STEP = 32
SUB_POS = 512
MIB = 1024 * 1024


def _params(sem, vmem_mib=48):
    return pltpu.CompilerParams(dimension_semantics=sem, vmem_limit_bytes=vmem_mib * MIB)


def _tile(n, pref):
    t = math.gcd(n, pref)
    assert t % 8 == 0 or t == n, (n, pref)
    return t


def _dot(a, b):
    return jnp.dot(a, b, preferred_element_type=F32)


def _dot_nt(a, b):
    return lax.dot_general(a, b, (((1,), (1,)), ((), ())), preferred_element_type=F32)


def _rms_scale(x, n):
    return lax.rsqrt(jnp.sum(x * x, axis=-1, keepdims=True) * (1.0 / n) + EPS)


def _rope_half(x, cos, sin):
    half = x.shape[-1] // 2
    x1, x2 = x[:, :half], x[:, half:]
    return jnp.concatenate([x1 * cos - x2 * sin, x1 * sin + x2 * cos], axis=-1)


def _mm_kernel(*refs, has_norm, has_res, has_into):
    it = iter(refs)
    x_ref = next(it)
    g_ref = next(it) if has_norm else None
    w_ref = next(it)
    r_ref = next(it) if has_res else None
    if has_into:
        next(it)
    o_ref = next(it)
    xb_ref = next(it)

    @pl.when(pl.program_id(1) == 0)
    def _():
        x = x_ref[...].astype(F32)
        if has_norm:
            x = x * _rms_scale(x, x.shape[-1]) * g_ref[...]
        xb_ref[...] = x.astype(BF16)

    acc = _dot(xb_ref[...], w_ref[...])
    if has_res:
        acc = r_ref[...] + acc
    o_ref[...] = acc.astype(o_ref.dtype)


def _matmul(x, w, *, g=None, res=None, row0=0, rows=None, into=None, tm, tn, name):
    M, K = x.shape
    N = w.shape[1]
    rows = M - row0 if rows is None else rows
    assert row0 % tm == 0 and rows % tm == 0
    rb0 = row0 // tm
    in_specs = [pl.BlockSpec((tm, K), lambda i, j: (rb0 + i, 0))]
    args = [x]
    if g is not None:
        in_specs.append(pl.BlockSpec((1, K), lambda i, j: (0, 0)))
        args.append(g.reshape(1, K))
    in_specs.append(pl.BlockSpec((K, tn), lambda i, j: (0, j)))
    args.append(w)
    if res is not None:
        in_specs.append(pl.BlockSpec((tm, tn), lambda i, j: (rb0 + i, j)))
        args.append(res)
    aliases = {}
    if into is not None:
        aliases = {len(args): 0}
        in_specs.append(pl.BlockSpec(memory_space=pl.ANY))
        args.append(into)
    return pl.pallas_call(
        functools.partial(_mm_kernel, has_norm=g is not None, has_res=res is not None, has_into=into is not None),
        grid=(rows // tm, N // tn),
        in_specs=in_specs,
        out_specs=pl.BlockSpec((tm, tn), lambda i, j: (rb0 + i, j)),
        out_shape=jax.ShapeDtypeStruct((M, N), F32),
        input_output_aliases=aliases,
        scratch_shapes=[pltpu.VMEM((tm, K), BF16)],
        compiler_params=_params(("parallel", "arbitrary")),
        name=name,
    )(*args)


def _matmul_streams(x, w, *, Mp, tm_p, tm_s, **kw):
    name = kw.pop("name")
    out = _matmul(x, w, rows=Mp, tm=tm_p, name=name + "_prompt", **kw)
    return _matmul(x, w, row0=Mp, into=out, tm=tm_s, name=name + "_decode", **kw)


def _ret_prompt_kernel(lg_ref, q_ref, k_ref, v_ref, g_ref, cos_ref, sin_ref, o_ref, s_out_ref, s_ref):
    hg = pl.program_id(1)
    c = pl.program_id(2)
    L = q_ref.shape[0]
    HB, dk, dv = s_ref.shape

    @pl.when(c == 0)
    def _():
        s_ref[...] = jnp.zeros_like(s_ref)

    cos, sin = cos_ref[...], sin_ref[...]
    ii = lax.broadcasted_iota(jnp.int32, (L, L), 0)
    jj = lax.broadcasted_iota(jnp.int32, (L, L), 1)
    rel = (ii - jj).astype(F32)
    ri = lax.broadcasted_iota(jnp.int32, (L, 1), 0).astype(F32)
    for hh in range(HB):
        lg = lg_ref[hg * HB + hh]
        q = _rope_half(q_ref[:, hh * dk:(hh + 1) * dk], cos, sin)
        k = _rope_half(k_ref[:, hh * dk:(hh + 1) * dk], cos, sin) * (dk ** -0.5)
        vb = v_ref[:, hh * dv:(hh + 1) * dv].astype(BF16)
        qb = q.astype(BF16)
        decay = jnp.where(rel >= 0, jnp.exp(lg * jnp.maximum(rel, 0.0)), 0.0)
        scores = _dot_nt(qb, k.astype(BF16)) * decay
        inner = _dot(scores.astype(BF16), vb)
        s_old = s_ref[hh]
        cross = _dot(qb, s_old.astype(BF16)) * jnp.exp((ri + 1.0) * lg)
        k_dec = k * jnp.exp((L - 1.0 - ri) * lg)
        g_all = jnp.exp(jnp.zeros((1, 1), F32) + L * lg)
        s_new = g_all * s_old + _dot(k_dec.T.astype(BF16), vb)
        s_ref[hh] = s_new
        o = inner + cross
        o = o * _rms_scale(o, dv)
        gate = g_ref[:, hh * dv:(hh + 1) * dv]
        o_ref[:, hh * dv:(hh + 1) * dv] = (gate * jax.nn.sigmoid(gate)) * o

    @pl.when(c == pl.num_programs(2) - 1)
    def _():
        s_out_ref[0] = s_ref[...]


def _retention_prompt(qkvg, log_g, cos, sin, *, B, T, H, dk, dv, M):
    C = RET_CHUNK if T % RET_CHUNK == 0 else T
    nC = T // C
    HB = math.gcd(H, 4)
    assert (2 * H * dk) % (HB * dv) == 0
    nH = H // HB
    voff = 2 * H * dk // (HB * dv)
    return pl.pallas_call(
        _ret_prompt_kernel,
        grid=(B, nH, nC),
        in_specs=[
            pl.BlockSpec(memory_space=pltpu.SMEM),
            pl.BlockSpec((C, HB * dk), lambda b, h, c: (b * nC + c, h)),
            pl.BlockSpec((C, HB * dk), lambda b, h, c: (b * nC + c, nH + h)),
            pl.BlockSpec((C, HB * dv), lambda b, h, c: (b * nC + c, voff + h)),
            pl.BlockSpec((C, HB * dv), lambda b, h, c: (b * nC + c, voff + nH + h)),
            pl.BlockSpec((C, dk // 2), lambda b, h, c: (c, 0)),
            pl.BlockSpec((C, dk // 2), lambda b, h, c: (c, 0)),
        ],
        out_specs=[
            pl.BlockSpec((C, HB * dv), lambda b, h, c: (b * nC + c, h)),
            pl.BlockSpec((1, HB, dk, dv), lambda b, h, c: (b, h, 0, 0)),
        ],
        out_shape=[
            jax.ShapeDtypeStruct((M, H * dv), F32),
            jax.ShapeDtypeStruct((B, H, dk, dv), F32),
        ],
        scratch_shapes=[pltpu.VMEM((HB, dk, dv), F32)],
        compiler_params=_params(("parallel", "parallel", "arbitrary")),
        name="retention_prompt",
    )(log_g, qkvg, qkvg, qkvg, qkvg, cos, sin)


def _ret_sample_kernel(lg_ref, q_ref, k_ref, v_ref, g_ref, cos_ref, sin_ref, s_in_ref, o_alias_ref,
                       o_ref, s_out_ref, *, T):
    del o_alias_ref
    hg = pl.program_id(1)
    R = q_ref.shape[0]
    HB, dk, dv = s_in_ref.shape[1:]
    cos, sin = cos_ref[...], sin_ref[...]
    ii = lax.broadcasted_iota(jnp.int32, (R, R), 0)
    jj = lax.broadcasted_iota(jnp.int32, (R, R), 1)
    rel = (ii - jj).astype(F32)
    same = (ii // T) == (jj // T)
    row = lax.broadcasted_iota(jnp.int32, (R, 1), 0)
    tok = (row % T).astype(F32)
    pad = RET_CHUNK - R
    for hh in range(HB):
        lg = lg_ref[hg * HB + hh]
        q = _rope_half(q_ref[:, hh * dk:(hh + 1) * dk], cos, sin)
        k = _rope_half(k_ref[:, hh * dk:(hh + 1) * dk], cos, sin) * (dk ** -0.5)
        v = v_ref[:, hh * dv:(hh + 1) * dv]
        qb = q.astype(BF16)
        decay = jnp.where(same, jnp.where(rel >= 0, jnp.exp(lg * jnp.maximum(rel, 0.0)), 0.0), 0.0)
        scores = _dot_nt(qb, k.astype(BF16)) * decay
        sc = scores.astype(BF16).astype(F32)
        vr = v.astype(BF16).astype(F32)
        inner = jnp.zeros((R, dv), F32)
        for j in range(R):
            inner = inner + sc[:, j:j + 1] * vr[j:j + 1, :]
        k_dec = k * jnp.exp((T - 1.0 - tok) * lg)
        g_all = jnp.exp(jnp.zeros((1, 1), F32) + T * lg)
        v_pad = jnp.concatenate([v, jnp.zeros((pad, dv), F32)], axis=0).astype(BF16)
        cross = jnp.zeros((R, dv), F32)
        for s in range(R // T):
            mine = (row // T) == s
            s_old = s_in_ref[s, hh]
            cross = jnp.where(mine, _dot(qb, s_old.astype(BF16)), cross)
            kd = jnp.where(mine, k_dec, 0.0)
            kd_t = jnp.concatenate([kd, jnp.zeros((pad, dk), F32)], axis=0).T.astype(BF16)
            s_out_ref[s, hh] = g_all * s_old + _dot(kd_t, v_pad)
        o = inner + cross * jnp.exp((tok + 1.0) * lg)
        o = o * _rms_scale(o, dv)
        gate = g_ref[:, hh * dv:(hh + 1) * dv]
        o_ref[:, hh * dv:(hh + 1) * dv] = (gate * jax.nn.sigmoid(gate)) * o


def _retention_sample(qkvg, log_g, cos, sin, state, gated, *, Bd, T, H, dk, dv, Mp):
    R = 8
    assert R % T == 0 and Bd % (R // T) == 0 and Mp % R == 0
    spb = R // T
    rb0 = Mp // R
    HB = math.gcd(H, 4)
    assert (2 * H * dk) % (HB * dv) == 0
    nH = H // HB
    voff = 2 * H * dk // (HB * dv)
    M = qkvg.shape[0]
    return pl.pallas_call(
        functools.partial(_ret_sample_kernel, T=T),
        grid=(Bd // spb, nH),
        in_specs=[
            pl.BlockSpec(memory_space=pltpu.SMEM),
            pl.BlockSpec((R, HB * dk), lambda i, h: (rb0 + i, h)),
            pl.BlockSpec((R, HB * dk), lambda i, h: (rb0 + i, nH + h)),
            pl.BlockSpec((R, HB * dv), lambda i, h: (rb0 + i, voff + h)),
            pl.BlockSpec((R, HB * dv), lambda i, h: (rb0 + i, voff + nH + h)),
            pl.BlockSpec((R, dk // 2), lambda i, h: (0, 0)),
            pl.BlockSpec((R, dk // 2), lambda i, h: (0, 0)),
            pl.BlockSpec((spb, HB, dk, dv), lambda i, h: (i, h, 0, 0)),
            pl.BlockSpec(memory_space=pl.ANY),
        ],
        out_specs=[
            pl.BlockSpec((R, HB * dv), lambda i, h: (rb0 + i, h)),
            pl.BlockSpec((spb, HB, dk, dv), lambda i, h: (i, h, 0, 0)),
        ],
        out_shape=[
            jax.ShapeDtypeStruct((M, H * dv), F32),
            jax.ShapeDtypeStruct((Bd, H, dk, dv), F32),
        ],
        input_output_aliases={8: 0},
        compiler_params=_params(("parallel", "parallel")),
        name="retention_sample",
    )(log_g, qkvg, qkvg, qkvg, qkvg, cos, sin, state, gated)


def _router_kernel(y_ref, g_ref, wr_ref, br_ref, route_ref, cnt_ref, carry_ref):
    i = pl.program_id(0)
    tm = y_ref.shape[0]
    NE = N_GROUPS * EXP_PER_GROUP

    @pl.when(i == 0)
    def _():
        carry_ref[...] = jnp.zeros_like(carry_ref)

    y = y_ref[...]
    hn = y * _rms_scale(y, y.shape[-1]) * g_ref[...]
    lt = lax.dot_general(wr_ref[...], hn, (((1,), (1,)), ((), ())),
                         precision=lax.Precision.HIGHEST, preferred_element_type=F32) + br_ref[...]
    gl = [lt[k:k + 1, :] for k in range(N_GROUPS)]
    gmax = functools.reduce(jnp.maximum, gl)
    grp = jnp.full(gmax.shape, N_GROUPS - 1, jnp.int32)
    for k in range(N_GROUPS - 2, -1, -1):
        grp = jnp.where(gl[k] == gmax, k, grp)
    pg = 1.0 / functools.reduce(lambda a, b: a + b, [jnp.exp(x - gmax) for x in gl])
    el = []
    for k in range(EXP_PER_GROUP):
        sel = lt[N_GROUPS + (N_GROUPS - 1) * EXP_PER_GROUP + k:N_GROUPS + (N_GROUPS - 1) * EXP_PER_GROUP + k + 1, :]
        for gi in range(N_GROUPS - 2, -1, -1):
            r = N_GROUPS + gi * EXP_PER_GROUP + k
            sel = jnp.where(grp == gi, lt[r:r + 1, :], sel)
        el.append(sel)
    v1 = functools.reduce(jnp.maximum, el)
    i1 = jnp.full(v1.shape, EXP_PER_GROUP - 1, jnp.int32)
    for k in range(EXP_PER_GROUP - 2, -1, -1):
        i1 = jnp.where(el[k] == v1, k, i1)
    el2 = [jnp.where(i1 == k, -jnp.inf, el[k]) for k in range(EXP_PER_GROUP)]
    v2 = functools.reduce(jnp.maximum, el2)
    i2 = jnp.full(v2.shape, EXP_PER_GROUP - 1, jnp.int32)
    for k in range(EXP_PER_GROUP - 2, -1, -1):
        i2 = jnp.where(el2[k] == v2, k, i2)
    e21 = jnp.exp(v2 - v1)
    den = 1.0 / (1.0 + e21)
    c1 = pg * den
    c2 = pg * (e21 * den)
    e1 = grp * EXP_PER_GROUP + i1
    e2 = grp * EXP_PER_GROUP + i2
    eid = lax.broadcasted_iota(jnp.int32, (NE, tm), 0)
    oh1 = (eid == e1).astype(F32)
    oh2 = (eid == e2).astype(F32)
    oh = oh1 + oh2
    tr = lax.broadcasted_iota(jnp.int32, (tm, tm), 0)
    tc = lax.broadcasted_iota(jnp.int32, (tm, tm), 1)
    before = jnp.where(tr < tc, 1.0, 0.0).astype(BF16)
    base = carry_ref[...] + _dot(oh.astype(BF16), before)
    r1 = jnp.sum(oh1 * base, axis=0, keepdims=True)
    r2 = jnp.sum(oh2 * base, axis=0, keepdims=True)
    total = carry_ref[...] + jnp.sum(oh, axis=1, keepdims=True)
    carry_ref[...] = total
    cnt_ref[...] = jnp.broadcast_to(total, cnt_ref.shape)
    route_ref[...] = jnp.concatenate(
        [e1.astype(F32), e2.astype(F32), c1, c2, r1, r2, jnp.zeros((2, tm), F32)], axis=0)


def _router(y, g, w_gr, b_gr, w_er, b_er, *, tm):
    M, D = y.shape
    NE = N_GROUPS * EXP_PER_GROUP
    rows = 32
    wr = jnp.zeros((rows, D), F32).at[:N_GROUPS].set(w_gr.T).at[N_GROUPS:N_GROUPS + NE].set(w_er.T)
    br = jnp.zeros((rows, 1), F32).at[:N_GROUPS, 0].set(b_gr).at[N_GROUPS:N_GROUPS + NE, 0].set(b_er)
    return pl.pallas_call(
        _router_kernel,
        grid=(M // tm,),
        in_specs=[
            pl.BlockSpec((tm, D), lambda i: (i, 0)),
            pl.BlockSpec((1, D), lambda i: (0, 0)),
            pl.BlockSpec((rows, D), lambda i: (0, 0)),
            pl.BlockSpec((rows, 1), lambda i: (0, 0)),
        ],
        out_specs=[
            pl.BlockSpec((8, tm), lambda i: (0, i)),
            pl.BlockSpec((NE, 128), lambda i: (0, 0)),
        ],
        out_shape=[
            jax.ShapeDtypeStruct((8, M), F32),
            jax.ShapeDtypeStruct((NE, 128), F32),
        ],
        scratch_shapes=[pltpu.VMEM((NE, 1), F32)],
        compiler_params=_params(("arbitrary",)),
        name="moe_router",
    )(y, g.reshape(1, D), wr, br)


def _cast_kernel(w1_ref, w3_ref, w2_ref, o1_ref, o3_ref, o2_ref):
    o1_ref[0] = w1_ref[0, 0].astype(BF16)
    o3_ref[0] = w3_ref[0, 0].astype(BF16)
    o2_ref[0] = w2_ref[0, 0].astype(BF16)


def _cast_expert_weights(w1, w3, w2, *, layer):
    _, NE, D, FF = w1.shape
    nc = 4
    r1, r2 = D // nc, FF // nc
    return pl.pallas_call(
        _cast_kernel,
        grid=(NE, nc),
        in_specs=[
            pl.BlockSpec((1, 1, r1, FF), lambda e, c: (layer, e, c, 0)),
            pl.BlockSpec((1, 1, r1, FF), lambda e, c: (layer, e, c, 0)),
            pl.BlockSpec((1, 1, r2, D), lambda e, c: (layer, e, c, 0)),
        ],
        out_specs=[
            pl.BlockSpec((1, r1, FF), lambda e, c: (e, c, 0)),
            pl.BlockSpec((1, r1, FF), lambda e, c: (e, c, 0)),
            pl.BlockSpec((1, r2, D), lambda e, c: (e, c, 0)),
        ],
        out_shape=[
            jax.ShapeDtypeStruct((NE, D, FF), BF16),
            jax.ShapeDtypeStruct((NE, D, FF), BF16),
            jax.ShapeDtypeStruct((NE, FF, D), BF16),
        ],
        compiler_params=_params(("parallel", "parallel")),
        name="moe_cast_weights",
    )(w1, w3, w2)


def _ffn_kernel(te_ref, nu_ref, x_ref, g_ref, w1_ref, w3_ref, w2_ref, o_ref, xb_ref):
    del te_ref
    i = pl.program_id(0)
    f = pl.program_id(1)
    used = i < nu_ref[0]

    @pl.when(jnp.logical_and(used, f == 0))
    def _():
        x = x_ref[...]
        xb_ref[...] = (x * _rms_scale(x, x.shape[-1]) * g_ref[...]).astype(BF16)

    @pl.when(used)
    def _():
        x = xb_ref[...]
        a = _dot(x, w1_ref[0])
        b = _dot(x, w3_ref[0])
        hh = (a * jax.nn.sigmoid(a)) * b
        y = _dot(hh.astype(BF16), w2_ref[0])

        @pl.when(f == 0)
        def _():
            o_ref[...] = y

        @pl.when(f > 0)
        def _():
            o_ref[...] += y

    @pl.when(jnp.logical_and(jnp.logical_not(used), f == 0))
    def _():
        o_ref[...] = jnp.zeros_like(o_ref)


def _expert_ffn(xs, g, tile_expert, n_used, w1, w3, w2, *, tm, tf):
    NS, D = xs.shape
    FF = w1.shape[2]
    nT = NS // tm

    def xi(i, f, te, nu):
        return (jnp.minimum(i, nu[0] - 1), 0)

    return pl.pallas_call(
        _ffn_kernel,
        grid_spec=pltpu.PrefetchScalarGridSpec(
            num_scalar_prefetch=2,
            grid=(nT, FF // tf),
            in_specs=[
                pl.BlockSpec((tm, D), xi),
                pl.BlockSpec((1, D), lambda i, f, te, nu: (0, 0)),
                pl.BlockSpec((1, D, tf), lambda i, f, te, nu: (te[i], 0, f)),
                pl.BlockSpec((1, D, tf), lambda i, f, te, nu: (te[i], 0, f)),
                pl.BlockSpec((1, tf, D), lambda i, f, te, nu: (te[i], f, 0)),
            ],
            out_specs=pl.BlockSpec((tm, D), lambda i, f, te, nu: (i, 0)),
            scratch_shapes=[pltpu.VMEM((tm, D), BF16)],
        ),
        out_shape=jax.ShapeDtypeStruct((NS, D), F32),
        compiler_params=_params(("arbitrary", "arbitrary")),
        name="moe_ffn",
    )(tile_expert, n_used, xs, g.reshape(1, D), w1, w3, w2)


def _combine_kernel(y_ref, a_ref, b_ref, c_ref, o_ref):
    c = c_ref[...]
    o_ref[...] = y_ref[...] + (c[:, 0:1] * a_ref[...] + c[:, 1:2] * b_ref[...])


def _combine(y, a, b, c, *, tm):
    M, D = y.shape
    row = pl.BlockSpec((tm, D), lambda i: (i, 0))
    return pl.pallas_call(
        _combine_kernel,
        grid=(M // tm,),
        in_specs=[row, row, row, pl.BlockSpec((tm, 2), lambda i: (i, 0))],
        out_specs=row,
        out_shape=jax.ShapeDtypeStruct((M, D), F32),
        compiler_params=_params(("parallel",)),
        name="moe_combine",
    )(y, a, b, c)


def _moe(y, g, w_gr, b_gr, w_er, b_er, w1, w3, w2, *, layer, tm):
    M, D = y.shape
    NE = N_GROUPS * EXP_PER_GROUP
    route, cnt = _router(y, g, w_gr, b_gr, w_er, b_er, tm=tm)
    e1 = route[0].astype(jnp.int32)
    e2 = route[1].astype(jnp.int32)
    r1 = route[4].astype(jnp.int32)
    r2 = route[5].astype(jnp.int32)
    counts = cnt[:, 0].astype(jnp.int32)
    padded = ((counts + tm - 1) // tm) * tm
    ends = jnp.cumsum(padded)
    off = ends - padded
    d1 = off[e1] + r1
    d2 = off[e2] + r2
    NS = ((2 * M + NE * (tm - 1)) // tm) * tm
    tok = jnp.arange(M, dtype=jnp.int32)
    src = (jnp.arange(NS, dtype=jnp.int32) % M).at[jnp.concatenate([d1, d2])].set(
        jnp.concatenate([tok, tok]), unique_indices=True, mode="promise_in_bounds")
    n_used = (ends[-1] // tm).astype(jnp.int32).reshape(1)
    tile_start = jnp.minimum(jnp.arange(NS // tm, dtype=jnp.int32) * tm, ends[-1] - tm)
    tile_expert = jnp.sum((ends[None, :] <= tile_start[:, None]).astype(jnp.int32), axis=1)
    tile_expert = jnp.minimum(tile_expert, NE - 1)
    xs = y.at[src].get(mode="promise_in_bounds")
    w1b, w3b, w2b = _cast_expert_weights(w1, w3, w2, layer=layer)
    tf = _tile(w1.shape[3], 512)
    ys = _expert_ffn(xs, g, tile_expert, n_used, w1b, w3b, w2b, tm=tm, tf=tf)
    a = ys.at[d1].get(mode="promise_in_bounds")
    b = ys.at[d2].get(mode="promise_in_bounds")
    return _combine(y, a, b, route[2:4].T, tm=tm)


def _latent_kernel(x_ref, gs_ref, w_ref, gc_ref, ckv_ref, kpe_ref, kpew_ref):
    x = x_ref[...]
    s = x * _rms_scale(x, x.shape[-1]) * gs_ref[...]
    ck = _dot(s.astype(BF16), w_ref[...])
    lora = ckv_ref.shape[1]
    wide = kpew_ref.shape[1]
    c = ck[:, :lora]
    ckv_ref[...] = c * _rms_scale(c, lora) * gc_ref[...]
    kpew_ref[...] = ck[:, lora:lora + wide]
    kpe_ref[...] = ck[:, lora + wide:]


def _latent(y, g_src, w_dkv, g_ckv, *, tm):
    M, D = y.shape
    lora = g_ckv.shape[0]
    w_pe = w_dkv[:, lora:]
    w_all = jnp.concatenate([w_dkv[:, :lora], _spread(w_pe), w_pe], axis=1).astype(BF16)
    N = w_all.shape[1]
    return pl.pallas_call(
        _latent_kernel,
        grid=(M // tm,),
        in_specs=[
            pl.BlockSpec((tm, D), lambda i: (i, 0)),
            pl.BlockSpec((1, D), lambda i: (0, 0)),
            pl.BlockSpec((D, N), lambda i: (0, 0)),
            pl.BlockSpec((1, lora), lambda i: (0, 0)),
        ],
        out_specs=[
            pl.BlockSpec((tm, lora), lambda i: (i, 0)),
            pl.BlockSpec((tm, ROPE_DIM), lambda i: (i, 0)),
            pl.BlockSpec((tm, 2 * ROPE_DIM), lambda i: (i, 0)),
        ],
        out_shape=[
            jax.ShapeDtypeStruct((M, lora), F32),
            jax.ShapeDtypeStruct((M, ROPE_DIM), F32),
            jax.ShapeDtypeStruct((M, 2 * ROPE_DIM), F32),
        ],
        compiler_params=_params(("parallel",)),
        name="shared_latent",
    )(y, g_src.reshape(1, D), w_all, g_ckv.reshape(1, lora))


def _spread(x):
    half = ROPE_DIM // 2
    z = jnp.zeros(x.shape[:-1] + (half,), x.dtype)
    return jnp.concatenate([x[..., :half], z, x[..., half:], z], axis=-1)


def _head_norm_rope(nope, pe, gain, cw, sw):
    ssq = jnp.sum(nope * nope + pe * pe, axis=-1, keepdims=True)
    inv = lax.rsqrt(ssq * (1.0 / QK_DIM) + EPS)
    n = nope * inv * gain[:, :NOPE_DIM]
    p = pe * inv * gain[:, NOPE_DIM:]
    p = p * cw + pltpu.roll(p, ROPE_DIM, axis=1) * sw
    return jnp.concatenate([n, p], axis=-1)


def _kv_prep_kernel(ckv_ref, kpe_ref, wuk_ref, wuv_ref, gk_ref, cos_ref, sin_ref, k_ref, v_ref):
    cb = ckv_ref[...].astype(BF16)
    kn = _dot(cb, wuk_ref[...])
    v_ref[...] = _dot(cb, wuv_ref[...]).astype(v_ref.dtype)
    pe, gain, cw, sw = kpe_ref[...], gk_ref[...], cos_ref[...], sin_ref[...]
    for hh in range(k_ref.shape[0]):
        k_ref[hh] = _head_norm_rope(kn[:, hh * NOPE_DIM:(hh + 1) * NOPE_DIM], pe, gain, cw, sw).astype(k_ref.dtype)


def _kv_prep(ckv, kpe_wide, w_uk, w_uv, g_wide, cw, sw, *, rows, H, tm):
    lora = ckv.shape[1]
    HB = math.gcd(H, 4)
    return pl.pallas_call(
        _kv_prep_kernel,
        grid=(rows // tm, H // HB),
        in_specs=[
            pl.BlockSpec((tm, lora), lambda i, h: (i, 0)),
            pl.BlockSpec((tm, 2 * ROPE_DIM), lambda i, h: (i, 0)),
            pl.BlockSpec((lora, HB * NOPE_DIM), lambda i, h: (0, h)),
            pl.BlockSpec((lora, HB * V_DIM), lambda i, h: (0, h)),
            pl.BlockSpec((1, QK_PAD), lambda i, h: (0, 0)),
            pl.BlockSpec((tm, 2 * ROPE_DIM), lambda i, h: (i, 0)),
            pl.BlockSpec((tm, 2 * ROPE_DIM), lambda i, h: (i, 0)),
        ],
        out_specs=[
            pl.BlockSpec((HB, tm, QK_PAD), lambda i, h: (h, i, 0)),
            pl.BlockSpec((tm, HB * V_DIM), lambda i, h: (i, h)),
        ],
        out_shape=[
            jax.ShapeDtypeStruct((H, rows, QK_PAD), BF16),
            jax.ShapeDtypeStruct((rows, H * V_DIM), BF16),
        ],
        compiler_params=_params(("parallel", "arbitrary")),
        name="mla_kv_prep",
    )(ckv, kpe_wide, w_uk, w_uv, g_wide.reshape(1, QK_PAD), cw, sw)


def _q_prep_kernel(q_ref, gq_ref, cos_ref, sin_ref, o_ref):
    q = q_ref[...]
    o_ref[0] = _head_norm_rope(q[:, :NOPE_DIM], q[:, NOPE_DIM:], gq_ref[...],
                               cos_ref[...], sin_ref[...]).astype(o_ref.dtype)


def _q_prep(qraw, g_wide, cw, sw, *, H, tm):
    M = qraw.shape[0]
    return pl.pallas_call(
        _q_prep_kernel,
        grid=(M // tm, H),
        in_specs=[
            pl.BlockSpec((tm, QK_PAD), lambda i, h: (i, h)),
            pl.BlockSpec((1, QK_PAD), lambda i, h: (0, 0)),
            pl.BlockSpec((tm, 2 * ROPE_DIM), lambda i, h: (i, 0)),
            pl.BlockSpec((tm, 2 * ROPE_DIM), lambda i, h: (i, 0)),
        ],
        out_specs=pl.BlockSpec((1, tm, QK_PAD), lambda i, h: (h, i, 0)),
        out_shape=jax.ShapeDtypeStruct((H, M, QK_PAD), BF16),
        compiler_params=_params(("parallel", "arbitrary")),
        name="mla_q_prep",
    )(qraw, g_wide.reshape(1, QK_PAD), cw, sw)


def _flash_kernel(q_ref, k_ref, v_ref, o_ref, m_ref, acc_ref):
    qi = pl.program_id(2)
    HB, tq, _ = q_ref.shape
    scale = QK_DIM ** -0.5
    m_ref[...] = jnp.full_like(m_ref, -jnp.inf)
    acc_ref[...] = jnp.zeros_like(acc_ref)
    ones = jnp.ones((tq, V_DIM), BF16)

    def step(start, masked):
        for h in range(HB):
            k = k_ref[h, pl.ds(start, tq), :]
            v = jnp.concatenate([v_ref[pl.ds(start, tq), h * V_DIM:(h + 1) * V_DIM], ones], axis=1)
            s = _dot_nt(q_ref[h], k) * scale
            if masked:
                row = lax.broadcasted_iota(jnp.int32, (tq, tq), 0)
                col = lax.broadcasted_iota(jnp.int32, (tq, tq), 1)
                s = jnp.where(col <= row, s, -jnp.inf)
            m_old = m_ref[h]
            m_new = jnp.maximum(m_old, jnp.max(s, axis=-1, keepdims=True))
            p = jnp.exp(s - m_new)
            acc_ref[h] = jnp.exp(m_old - m_new) * acc_ref[h] + _dot(p.astype(BF16), v)
            m_ref[h] = m_new

    def full_tile(ki, carry):
        step(pl.multiple_of(ki * tq, tq), False)
        return carry

    lax.fori_loop(0, qi, full_tile, 0)
    step(pl.multiple_of(qi * tq, tq), True)
    for h in range(HB):
        acc = acc_ref[h]
        o_ref[:, h * V_DIM:(h + 1) * V_DIM] = acc[:, :V_DIM] / acc[:, V_DIM:V_DIM + 1]


def _flash_prompt(qpad, kpad, v, *, B, T, H, M, tq):
    nq = T // tq
    HB = math.gcd(H, 2)
    return pl.pallas_call(
        _flash_kernel,
        grid=(B, H // HB, nq),
        in_specs=[
            pl.BlockSpec((HB, tq, QK_PAD), lambda b, h, qi: (h, b * nq + qi, 0)),
            pl.BlockSpec((HB, T, QK_PAD), lambda b, h, qi: (h, b, 0)),
            pl.BlockSpec((T, HB * V_DIM), lambda b, h, qi: (b, h)),
        ],
        out_specs=pl.BlockSpec((tq, HB * V_DIM), lambda b, h, qi: (b * nq + qi, h)),
        out_shape=jax.ShapeDtypeStruct((M, H * V_DIM), F32),
        scratch_shapes=[pltpu.VMEM((HB, tq, 1), F32), pltpu.VMEM((HB, tq, 2 * V_DIM), F32)],
        compiler_params=_params(("parallel", "parallel", "arbitrary")),
        name="mla_prompt_attention",
    )(qpad, kpad, v)


def _absorb_kernel(q_ref, wuk_ref, gk_ref, o_ref):
    qg = q_ref[0][:, :NOPE_DIM].astype(F32) * gk_ref[...]
    o_ref[0] = _dot_nt(qg.astype(BF16), wuk_ref[...]).astype(o_ref.dtype)


def _absorb(qpad, w_uk, g_k, *, H, Mp, Ms):
    lora = w_uk.shape[0]
    assert Mp % Ms == 0
    return pl.pallas_call(
        _absorb_kernel,
        grid=(H,),
        in_specs=[
            pl.BlockSpec((1, Ms, QK_PAD), lambda h: (h, Mp // Ms, 0)),
            pl.BlockSpec((lora, NOPE_DIM), lambda h: (0, h)),
            pl.BlockSpec((1, NOPE_DIM), lambda h: (0, 0)),
        ],
        out_specs=pl.BlockSpec((1, Ms, lora), lambda h: (h, 0, 0)),
        out_shape=jax.ShapeDtypeStruct((H, Ms, lora), BF16),
        compiler_params=_params(("parallel",)),
        name="mla_absorb_queries",
    )(qpad, w_uk, g_k[:NOPE_DIM].reshape(1, NOPE_DIM))


def _decode_kernel(pt_ref, qabs_ref, qpe_ref, wt_ref, *refs, T, H, P):
    del pt_ref
    ckv_pages = refs[:P]
    kpe_pages = refs[P:2 * P]
    (cnew_ref, knew_ref, cos_ref, sin_ref, cosn_ref, sinn_ref, g1_ref, g2_ref,
     o_ref, m_ref, l_ref, acc_ref) = refs[2 * P:]
    b = pl.program_id(0)
    t = pl.program_id(1)
    last = pl.num_programs(1) - 1
    page = ckv_pages[0].shape[1]
    lora = ckv_pages[0].shape[2]
    R = T * H
    NR = cnew_ref.shape[0]
    pps = SUB_POS // page

    @pl.when(t == 0)
    def _():
        m_ref[...] = jnp.full_like(m_ref, -jnp.inf)
        l_ref[...] = jnp.zeros_like(l_ref)
        acc_ref[...] = jnp.zeros_like(acc_ref)

    def scores(cb, kpe_t, cos, sin):
        W = cb.shape[0]
        kn = _dot_nt(wt_ref[...], cb)
        ssq = jnp.sum((kn * kn).reshape(H, NOPE_DIM, W), axis=1)
        sn = _dot_nt(qabs_ref[0], cb)
        ssq = ssq + jnp.sum(kpe_t * kpe_t, axis=0, keepdims=True)
        inv = lax.rsqrt(ssq * (1.0 / QK_DIM) + EPS)
        x1 = kpe_t[:ROPE_DIM // 2] * g1_ref[...]
        x2 = kpe_t[ROPE_DIM // 2:] * g2_ref[...]
        kr = jnp.concatenate([x1 * cos - x2 * sin, x1 * sin + x2 * cos], axis=0)
        sr = _dot(qpe_ref[0], kr.astype(BF16))
        return (sn + sr) * jnp.concatenate([inv] * T, axis=0) * (QK_DIM ** -0.5)

    def softmax_update(s, cb):
        m_old = m_ref[...]
        m_new = jnp.maximum(m_old, jnp.max(s, axis=-1, keepdims=True))
        alpha = jnp.exp(m_old - m_new)
        pr = jnp.exp(s - m_new)
        l_ref[...] = alpha * l_ref[...] + jnp.sum(pr, axis=-1, keepdims=True)
        acc_ref[...] = alpha * acc_ref[...] + _dot(pr.astype(BF16), cb)
        m_ref[...] = m_new

    cbs, ss = [], []
    for j in range(P // pps):
        pages = range(j * pps, (j + 1) * pps)
        cb = jnp.concatenate([ckv_pages[p][0].astype(BF16) for p in pages], axis=0)
        kpe_t = jnp.concatenate([kpe_pages[p][0] for p in pages], axis=1)
        cbs.append(cb)
        ss.append(scores(cb, kpe_t, cos_ref[:, j * SUB_POS:(j + 1) * SUB_POS],
                         sin_ref[:, j * SUB_POS:(j + 1) * SUB_POS]))
    softmax_update(jnp.concatenate(ss, axis=1), jnp.concatenate(cbs, axis=0))

    @pl.when(t == last)
    def _():
        cb = jnp.concatenate([cnew_ref[...], jnp.zeros((page - NR, lora), F32)], axis=0).astype(BF16)
        q_tok = lax.broadcasted_iota(jnp.int32, (R, page), 0) // H
        lane = lax.broadcasted_iota(jnp.int32, (R, page), 1)
        valid = jnp.logical_and(lane // T == b % (NR // T), lane % T <= q_tok)
        valid = jnp.logical_and(valid, lane < NR)
        s = scores(cb, knew_ref[0], cosn_ref[...], sinn_ref[...])
        softmax_update(jnp.where(valid, s, -jnp.inf), cb)
        o_ref[0] = acc_ref[...] / l_ref[...]


def _decode_attention(qabs, qpe, w_uk_t, cache_ckv, cache_kpe_t, page_table, ckv_all, kpe_new_t,
                      cos_t, sin_t, g_k, *, Bd, T, H, Mp):
    n_pages = page_table.shape[1]
    P = math.gcd(n_pages, PAGES_PER_STEP)
    page = cache_ckv.shape[1]
    lora = cache_ckv.shape[2]
    assert n_pages % P == 0 and SUB_POS % page == 0 and (P * page) % SUB_POS == 0
    steps = n_pages // P
    R = T * H
    NR = 8
    assert NR % T == 0 and Mp % NR == 0
    spb = NR // T
    pt = page_table.reshape(-1).astype(jnp.int32)
    half = ROPE_DIM // 2
    g1 = g_k[NOPE_DIM:NOPE_DIM + half].reshape(half, 1)
    g2 = g_k[NOPE_DIM + half:].reshape(half, 1)

    def page_spec(p, shape):
        def idx(b, t, pt):
            return (pt[b * n_pages + t * P + p], 0, 0)
        return pl.BlockSpec((1,) + shape, idx)

    in_specs = [
        pl.BlockSpec((1, R, lora), lambda b, t, pt: (b, 0, 0)),
        pl.BlockSpec((1, R, ROPE_DIM), lambda b, t, pt: (b, 0, 0)),
        pl.BlockSpec(w_uk_t.shape, lambda b, t, pt: (0, 0)),
    ]
    in_specs += [page_spec(p, (page, lora)) for p in range(P)]
    in_specs += [page_spec(p, (ROPE_DIM, page)) for p in range(P)]
    in_specs += [
        pl.BlockSpec((NR, lora), lambda b, t, pt: (Mp // NR + b // spb, 0)),
        pl.BlockSpec((1, ROPE_DIM, page), lambda b, t, pt: (b // spb, 0, 0)),
        pl.BlockSpec((half, P * page), lambda b, t, pt: (0, t)),
        pl.BlockSpec((half, P * page), lambda b, t, pt: (0, t)),
        pl.BlockSpec((half, page), lambda b, t, pt: (0, n_pages)),
        pl.BlockSpec((half, page), lambda b, t, pt: (0, n_pages)),
        pl.BlockSpec((half, 1), lambda b, t, pt: (0, 0)),
        pl.BlockSpec((half, 1), lambda b, t, pt: (0, 0)),
    ]
    return pl.pallas_call(
        functools.partial(_decode_kernel, T=T, H=H, P=P),
        grid_spec=pltpu.PrefetchScalarGridSpec(
            num_scalar_prefetch=1,
            grid=(Bd, steps),
            in_specs=in_specs,
            out_specs=pl.BlockSpec((1, R, lora), lambda b, t, pt: (b, 0, 0)),
            scratch_shapes=[
                pltpu.VMEM((R, 1), F32),
                pltpu.VMEM((R, 1), F32),
                pltpu.VMEM((R, lora), F32),
            ],
        ),
        out_shape=jax.ShapeDtypeStruct((Bd, R, lora), F32),
        compiler_params=_params(("parallel", "arbitrary")),
        name="mla_decode_attention",
    )(pt, qabs, qpe, w_uk_t, *([cache_ckv] * P), *([cache_kpe_t] * P), ckv_all, kpe_new_t, cos_t, sin_t, cos_t, sin_t,
      g1, g2)


def _head_values_kernel(c_ref, w_ref, alias_ref, o_ref):
    del alias_ref
    o_ref[...] = _dot(c_ref[...].astype(BF16), w_ref[...])


def _head_values(ctx, w_uv, attn, *, H, Mp, Ms):
    lora = w_uv.shape[0]
    return pl.pallas_call(
        _head_values_kernel,
        grid=(H,),
        in_specs=[
            pl.BlockSpec((Ms, lora), lambda h: (0, h)),
            pl.BlockSpec((lora, V_DIM), lambda h: (0, h)),
            pl.BlockSpec(memory_space=pl.ANY),
        ],
        out_specs=pl.BlockSpec((Ms, V_DIM), lambda h: (Mp // Ms, h)),
        out_shape=jax.ShapeDtypeStruct(attn.shape, F32),
        input_output_aliases={2: 0},
        compiler_params=_params(("parallel",)),
        name="mla_decode_values",
    )(ctx, w_uv, attn)


def _rope_tables(pos, d):
    freq = ROPE_THETA ** (-jnp.arange(0, d, 2, dtype=F32) / d)
    ang = pos.astype(F32)[:, None] * freq[None, :]
    return jnp.cos(ang), jnp.sin(ang)


def kernel(x_prompt, x_sample, state_ret, cache_ckv, cache_kpe, page_table, ln_mix, ln_ffn, ret_w_in, ret_w_out,
           kv_src_norm, w_dkv, ckv_norm, w_ukv, k_norm, mla_w_q, q_norm, mla_w_o, moe_w_gr, moe_b_gr, moe_w_er,
           moe_b_er, moe_w1, moe_w3, moe_w2):
    B, T, D = x_prompt.shape
    Bd, Td, _ = x_sample.shape
    n_a = state_ret.shape[0]
    depth = ln_mix.shape[0]
    RH, dk, dv = state_ret.shape[2:]
    lora, MH = w_ukv.shape[0], w_ukv.shape[1]
    Mp, Ms = B * T, Bd * Td
    M = Mp + Ms
    past = page_table.shape[1] * cache_ckv.shape[1]
    tm = _tile(math.gcd(Mp, Ms), 512)
    tm_p = _tile(Mp, 1024)

    pos_p = jnp.arange(T)
    pos_s = past + jnp.arange(Td)
    log_g = jnp.log1p(-jnp.exp2(-5.0 - jnp.arange(RH, dtype=F32)))
    y = jnp.concatenate([x_prompt.reshape(Mp, D), x_sample.reshape(Ms, D)], axis=0)

    ret_p, ret_s = [], []
    ckv = kpe = kpad = vals = None
    for i in range(depth):
        if i == n_a:
            ckv, kpe, kpe_wide = _latent(y, kv_src_norm, w_dkv, ckv_norm, tm=tm)
            pos_all = jnp.concatenate([jnp.tile(pos_p, B), jnp.tile(pos_s, Bd)])
            cos_m, sin_m = _rope_tables(pos_all, ROPE_DIM)
            cw = _spread(jnp.concatenate([cos_m, cos_m], axis=1))
            sw = _spread(jnp.concatenate([-sin_m, sin_m], axis=1))
            w_uk = w_ukv[..., :NOPE_DIM].reshape(lora, MH * NOPE_DIM).astype(BF16)
            w_uv = w_ukv[..., NOPE_DIM:].reshape(lora, MH * V_DIM).astype(BF16)
            gk_wide = jnp.concatenate([k_norm[:NOPE_DIM], _spread(k_norm[NOPE_DIM:])])
            kpad, vals = _kv_prep(ckv, kpe_wide, w_uk, w_uv, gk_wide, cw, sw, rows=Mp, H=MH, tm=tm)
        if i < n_a:
            w_in = ret_w_in[i].astype(BF16)
            qkvg = _matmul_streams(y, w_in, g=ln_mix[i], Mp=Mp, tm_p=tm_p, tm_s=tm,
                                   tn=_tile(w_in.shape[1], 1024), name="ret_in_proj")
            cos_p, sin_p = _rope_tables(pos_p, dk)
            gated, s_p = _retention_prompt(qkvg, log_g, cos_p, sin_p, B=B, T=T, H=RH, dk=dk, dv=dv, M=M)
            cos_s, sin_s = _rope_tables(jnp.tile(pos_s, 8 // Td), dk)
            gated, s_s = _retention_sample(qkvg, log_g, cos_s, sin_s, state_ret[i], gated,
                                           Bd=Bd, T=Td, H=RH, dk=dk, dv=dv, Mp=Mp)
            ret_p.append(s_p)
            ret_s.append(s_s)
            y = _matmul(gated, ret_w_out[i].astype(BF16), res=y, tm=tm, tn=_tile(D, 512), name="ret_out_proj")
        else:
            j = i - n_a
            wq = mla_w_q[j].reshape(D, MH, QK_DIM)
            wq = jnp.concatenate([wq[..., :NOPE_DIM], _spread(wq[..., NOPE_DIM:])], axis=-1)
            wq = wq.reshape(D, MH * QK_PAD).astype(BF16)
            qraw = _matmul_streams(y, wq, g=ln_mix[i], Mp=Mp, tm_p=tm_p, tm_s=tm,
                                   tn=_tile(MH * QK_PAD, 1024), name="mla_q_proj")
            gq_wide = jnp.concatenate([q_norm[j][:NOPE_DIM], _spread(q_norm[j][NOPE_DIM:])])
            qpad = _q_prep(qraw, gq_wide, cw, sw, H=MH, tm=tm)
            attn = _flash_prompt(qpad, kpad, vals, B=B, T=T, H=MH, M=M, tq=_tile(T, 512))
            qabs = _absorb(qpad, w_uk, k_norm, H=MH, Mp=Mp, Ms=Ms)
            qabs = qabs.reshape(MH, Bd, Td, lora).transpose(1, 2, 0, 3).reshape(Bd, Td * MH, lora)
            half = ROPE_DIM // 2
            qpe = jnp.concatenate([qpad[:, Mp:, NOPE_DIM:NOPE_DIM + half],
                                   qpad[:, Mp:, NOPE_DIM + ROPE_DIM:NOPE_DIM + ROPE_DIM + half]], axis=-1)
            qpe = qpe.reshape(MH, Bd, Td, ROPE_DIM).transpose(1, 2, 0, 3).reshape(Bd, Td * MH, ROPE_DIM)
            page = cache_ckv.shape[1]
            col = jnp.arange(past + page)
            pos_cols = jnp.where(col < past, col, past + (col - past) % Td)
            cos_c, sin_c = _rope_tables(pos_cols, ROPE_DIM)
            w_uk_t = w_uk.T
            spb = 8 // Td
            kpe_new_t = kpe[Mp:].reshape(Bd // spb, spb * Td, ROPE_DIM).swapaxes(1, 2)
            kpe_new_t = jnp.pad(kpe_new_t, ((0, 0), (0, 0), (0, page - spb * Td)))
            ctx = _decode_attention(qabs, qpe, w_uk_t, cache_ckv, cache_kpe.swapaxes(1, 2), page_table, ckv,
                                    kpe_new_t, cos_c.T, sin_c.T, k_norm, Bd=Bd, T=Td, H=MH, Mp=Mp)
            attn = _head_values(ctx.reshape(Ms, MH * lora), w_uv, attn, H=MH, Mp=Mp, Ms=Ms)
            y = _matmul_streams(attn, mla_w_o[j].astype(BF16), res=y, Mp=Mp, tm_p=tm_p, tm_s=tm,
                                tn=_tile(D, 512), name="mla_out_proj")
        y = _moe(y, ln_ffn[i], moe_w_gr[i], moe_b_gr[i], moe_w_er[i], moe_b_er[i],
                 moe_w1, moe_w3, moe_w2, layer=i, tm=tm)

    yp = y[:Mp].reshape(B, T, D)
    ys = y[Mp:].reshape(Bd, Td, D)
    return (yp, ys, jnp.stack(ret_p, axis=0), jnp.stack(ret_s, axis=0),
            ckv[:Mp].reshape(B, T, lora), kpe[:Mp].reshape(B, T, ROPE_DIM),
            ckv[Mp:].reshape(Bd, Td, lora), kpe[Mp:].reshape(Bd, Td, ROPE_DIM))
```

```python
import functools
import math

import jax
import jax.numpy as jnp
from jax import lax
from jax.experimental import pallas as pl
from jax.experimental.pallas import tpu as pltpu

F32 = jnp.float32
BF16 = jnp.bfloat16
EPS = 1e-6
ROPE_THETA = 10000.0
RET_CHUNK = 128
NOPE_DIM = 128
ROPE_DIM = 64
V_DIM = 128
QK_DIM = NOPE_DIM + ROPE_DIM
QK_PAD = 256
N_GROUPS = 4
EXP_PER_GROUP = 4
PAGES_PER_STEP = 32
SUB_POS = 512
MIB = 1024 * 1024


def _params(sem, vmem_mib=48):
    return pltpu.CompilerParams(dimension_semantics=sem, vmem_limit_bytes=vmem_mib * MIB)


def _tile(n, pref):
    t = math.gcd(n, pref)
    assert t % 8 == 0 or t == n, (n, pref)
    return t


def _dot(a, b):
    return jnp.dot(a, b, preferred_element_type=F32)


def _dot_nt(a, b):
    return lax.dot_general(a, b, (((1,), (1,)), ((), ())), preferred_element_type=F32)


def _rms_scale(x, n):
    return lax.rsqrt(jnp.sum(x * x, axis=-1, keepdims=True) * (1.0 / n) + EPS)


def _rope_half(x, cos, sin):
    half = x.shape[-1] // 2
    x1, x2 = x[:, :half], x[:, half:]
    return jnp.concatenate([x1 * cos - x2 * sin, x1 * sin + x2 * cos], axis=-1)


def _mm_kernel(*refs, has_norm, has_res, has_into):
    it = iter(refs)
    x_ref = next(it)
    g_ref = next(it) if has_norm else None
    w_ref = next(it)
    r_ref = next(it) if has_res else None
    if has_into:
        next(it)
    o_ref = next(it)
    xb_ref = next(it)

    @pl.when(pl.program_id(1) == 0)
    def _():
        x = x_ref[...].astype(F32)
        if has_norm:
            x = x * _rms_scale(x, x.shape[-1]) * g_ref[...]
        xb_ref[...] = x.astype(BF16)

    acc = _dot(xb_ref[...], w_ref[...])
    if has_res:
        acc = r_ref[...] + acc
    o_ref[...] = acc.astype(o_ref.dtype)


def _matmul(x, w, *, g=None, res=None, row0=0, rows=None, into=None, tm, tn, name):
    M, K = x.shape
    N = w.shape[1]
    rows = M - row0 if rows is None else rows
    assert row0 % tm == 0 and rows % tm == 0
    rb0 = row0 // tm
    in_specs = [pl.BlockSpec((tm, K), lambda i, j: (rb0 + i, 0))]
    args = [x]
    if g is not None:
        in_specs.append(pl.BlockSpec((1, K), lambda i, j: (0, 0)))
        args.append(g.reshape(1, K))
    in_specs.append(pl.BlockSpec((K, tn), lambda i, j: (0, j)))
    args.append(w)
    if res is not None:
        in_specs.append(pl.BlockSpec((tm, tn), lambda i, j: (rb0 + i, j)))
        args.append(res)
    aliases = {}
    if into is not None:
        aliases = {len(args): 0}
        in_specs.append(pl.BlockSpec(memory_space=pl.ANY))
        args.append(into)
    return pl.pallas_call(
        functools.partial(_mm_kernel, has_norm=g is not None, has_res=res is not None, has_into=into is not None),
        grid=(rows // tm, N // tn),
        in_specs=in_specs,
        out_specs=pl.BlockSpec((tm, tn), lambda i, j: (rb0 + i, j)),
        out_shape=jax.ShapeDtypeStruct((M, N), F32),
        input_output_aliases=aliases,
        scratch_shapes=[pltpu.VMEM((tm, K), BF16)],
        compiler_params=_params(("parallel", "arbitrary")),
        name=name,
    )(*args)


def _matmul_streams(x, w, *, Mp, tm_p, tm_s, **kw):
    name = kw.pop("name")
    out = _matmul(x, w, rows=Mp, tm=tm_p, name=name + "_prompt", **kw)
    return _matmul(x, w, row0=Mp, into=out, tm=tm_s, name=name + "_decode", **kw)


def _ret_prompt_kernel(lg_ref, q_ref, k_ref, v_ref, g_ref, cos_ref, sin_ref, o_ref, s_out_ref, s_ref):
    hg = pl.program_id(1)
    c = pl.program_id(2)
    L = q_ref.shape[0]
    HB, dk, dv = s_ref.shape

    @pl.when(c == 0)
    def _():
        s_ref[...] = jnp.zeros_like(s_ref)

    cos, sin = cos_ref[...], sin_ref[...]
    ii = lax.broadcasted_iota(jnp.int32, (L, L), 0)
    jj = lax.broadcasted_iota(jnp.int32, (L, L), 1)
    rel = (ii - jj).astype(F32)
    ri = lax.broadcasted_iota(jnp.int32, (L, 1), 0).astype(F32)
    for hh in range(HB):
        lg = lg_ref[hg * HB + hh]
        q = _rope_half(q_ref[:, hh * dk:(hh + 1) * dk], cos, sin)
        k = _rope_half(k_ref[:, hh * dk:(hh + 1) * dk], cos, sin) * (dk ** -0.5)
        vb = v_ref[:, hh * dv:(hh + 1) * dv].astype(BF16)
        qb = q.astype(BF16)
        decay = jnp.where(rel >= 0, jnp.exp(lg * jnp.maximum(rel, 0.0)), 0.0)
        scores = _dot_nt(qb, k.astype(BF16)) * decay
        inner = _dot(scores.astype(BF16), vb)
        s_old = s_ref[hh]
        cross = _dot(qb, s_old.astype(BF16)) * jnp.exp((ri + 1.0) * lg)
        k_dec = k * jnp.exp((L - 1.0 - ri) * lg)
        g_all = jnp.exp(jnp.zeros((1, 1), F32) + L * lg)
        s_new = g_all * s_old + _dot(k_dec.T.astype(BF16), vb)
        s_ref[hh] = s_new
        o = inner + cross
        o = o * _rms_scale(o, dv)
        gate = g_ref[:, hh * dv:(hh + 1) * dv]
        o_ref[:, hh * dv:(hh + 1) * dv] = (gate * jax.nn.sigmoid(gate)) * o

    @pl.when(c == pl.num_programs(2) - 1)
    def _():
        s_out_ref[0] = s_ref[...]


def _retention_prompt(qkvg, log_g, cos, sin, *, B, T, H, dk, dv, M):
    C = RET_CHUNK if T % RET_CHUNK == 0 else T
    nC = T // C
    HB = math.gcd(H, 4)
    assert (2 * H * dk) % (HB * dv) == 0
    nH = H // HB
    voff = 2 * H * dk // (HB * dv)
    return pl.pallas_call(
        _ret_prompt_kernel,
        grid=(B, nH, nC),
        in_specs=[
            pl.BlockSpec(memory_space=pltpu.SMEM),
            pl.BlockSpec((C, HB * dk), lambda b, h, c: (b * nC + c, h)),
            pl.BlockSpec((C, HB * dk), lambda b, h, c: (b * nC + c, nH + h)),
            pl.BlockSpec((C, HB * dv), lambda b, h, c: (b * nC + c, voff + h)),
            pl.BlockSpec((C, HB * dv), lambda b, h, c: (b * nC + c, voff + nH + h)),
            pl.BlockSpec((C, dk // 2), lambda b, h, c: (c, 0)),
            pl.BlockSpec((C, dk // 2), lambda b, h, c: (c, 0)),
        ],
        out_specs=[
            pl.BlockSpec((C, HB * dv), lambda b, h, c: (b * nC + c, h)),
            pl.BlockSpec((1, HB, dk, dv), lambda b, h, c: (b, h, 0, 0)),
        ],
        out_shape=[
            jax.ShapeDtypeStruct((M, H * dv), F32),
            jax.ShapeDtypeStruct((B, H, dk, dv), F32),
        ],
        scratch_shapes=[pltpu.VMEM((HB, dk, dv), F32)],
        compiler_params=_params(("parallel", "parallel", "arbitrary")),
        name="retention_prompt",
    )(log_g, qkvg, qkvg, qkvg, qkvg, cos, sin)


def _ret_sample_kernel(lg_ref, q_ref, k_ref, v_ref, g_ref, cos_ref, sin_ref, s_in_ref, o_alias_ref,
                       o_ref, s_out_ref, *, T):
    del o_alias_ref
    hg = pl.program_id(1)
    R = q_ref.shape[0]
    HB, dk, dv = s_in_ref.shape[1:]
    cos, sin = cos_ref[...], sin_ref[...]
    ii = lax.broadcasted_iota(jnp.int32, (R, R), 0)
    jj = lax.broadcasted_iota(jnp.int32, (R, R), 1)
    rel = (ii - jj).astype(F32)
    same = (ii // T) == (jj // T)
    row = lax.broadcasted_iota(jnp.int32, (R, 1), 0)
    tok = (row % T).astype(F32)
    pad = RET_CHUNK - R
    for hh in range(HB):
        lg = lg_ref[hg * HB + hh]
        q = _rope_half(q_ref[:, hh * dk:(hh + 1) * dk], cos, sin)
        k = _rope_half(k_ref[:, hh * dk:(hh + 1) * dk], cos, sin) * (dk ** -0.5)
        v = v_ref[:, hh * dv:(hh + 1) * dv]
        qb = q.astype(BF16)
        decay = jnp.where(same, jnp.where(rel >= 0, jnp.exp(lg * jnp.maximum(rel, 0.0)), 0.0), 0.0)
        scores = _dot_nt(qb, k.astype(BF16)) * decay
        sc = scores.astype(BF16).astype(F32)
        vr = v.astype(BF16).astype(F32)
        inner = jnp.zeros((R, dv), F32)
        for j in range(R):
            inner = inner + sc[:, j:j + 1] * vr[j:j + 1, :]
        k_dec = k * jnp.exp((T - 1.0 - tok) * lg)
        g_all = jnp.exp(jnp.zeros((1, 1), F32) + T * lg)
        v_pad = jnp.concatenate([v, jnp.zeros((pad, dv), F32)], axis=0).astype(BF16)
        cross = jnp.zeros((R, dv), F32)
        for s in range(R // T):
            mine = (row // T) == s
            s_old = s_in_ref[s, hh]
            cross = jnp.where(mine, _dot(qb, s_old.astype(BF16)), cross)
            kd = jnp.where(mine, k_dec, 0.0)
            kd_t = jnp.concatenate([kd, jnp.zeros((pad, dk), F32)], axis=0).T.astype(BF16)
            s_out_ref[s, hh] = g_all * s_old + _dot(kd_t, v_pad)
        o = inner + cross * jnp.exp((tok + 1.0) * lg)
        o = o * _rms_scale(o, dv)
        gate = g_ref[:, hh * dv:(hh + 1) * dv]
        o_ref[:, hh * dv:(hh + 1) * dv] = (gate * jax.nn.sigmoid(gate)) * o


def _retention_sample(qkvg, log_g, cos, sin, state, gated, *, Bd, T, H, dk, dv, Mp):
    R = 8
    assert R % T == 0 and Bd % (R // T) == 0 and Mp % R == 0
    spb = R // T
    rb0 = Mp // R
    HB = math.gcd(H, 4)
    assert (2 * H * dk) % (HB * dv) == 0
    nH = H // HB
    voff = 2 * H * dk // (HB * dv)
    M = qkvg.shape[0]
    return pl.pallas_call(
        functools.partial(_ret_sample_kernel, T=T),
        grid=(Bd // spb, nH),
        in_specs=[
            pl.BlockSpec(memory_space=pltpu.SMEM),
            pl.BlockSpec((R, HB * dk), lambda i, h: (rb0 + i, h)),
            pl.BlockSpec((R, HB * dk), lambda i, h: (rb0 + i, nH + h)),
            pl.BlockSpec((R, HB * dv), lambda i, h: (rb0 + i, voff + h)),
            pl.BlockSpec((R, HB * dv), lambda i, h: (rb0 + i, voff + nH + h)),
            pl.BlockSpec((R, dk // 2), lambda i, h: (0, 0)),
            pl.BlockSpec((R, dk // 2), lambda i, h: (0, 0)),
            pl.BlockSpec((spb, HB, dk, dv), lambda i, h: (i, h, 0, 0)),
            pl.BlockSpec(memory_space=pl.ANY),
        ],
        out_specs=[
            pl.BlockSpec((R, HB * dv), lambda i, h: (rb0 + i, h)),
            pl.BlockSpec((spb, HB, dk, dv), lambda i, h: (i, h, 0, 0)),
        ],
        out_shape=[
            jax.ShapeDtypeStruct((M, H * dv), F32),
            jax.ShapeDtypeStruct((Bd, H, dk, dv), F32),
        ],
        input_output_aliases={8: 0},
        compiler_params=_params(("parallel", "parallel")),
        name="retention_sample",
    )(log_g, qkvg, qkvg, qkvg, qkvg, cos, sin, state, gated)


def _router_kernel(y_ref, g_ref, wr_ref, br_ref, route_ref, cnt_ref, carry_ref):
    i = pl.program_id(0)
    tm = y_ref.shape[0]
    NE = N_GROUPS * EXP_PER_GROUP

    @pl.when(i == 0)
    def _():
        carry_ref[...] = jnp.zeros_like(carry_ref)

    y = y_ref[...]
    hn = y * _rms_scale(y, y.shape[-1]) * g_ref[...]
    lt = lax.dot_general(wr_ref[...], hn, (((1,), (1,)), ((), ())),
                         precision=lax.Precision.HIGHEST, preferred_element_type=F32) + br_ref[...]
    gl = [lt[k:k + 1, :] for k in range(N_GROUPS)]
    gmax = functools.reduce(jnp.maximum, gl)
    grp = jnp.full(gmax.shape, N_GROUPS - 1, jnp.int32)
    for k in range(N_GROUPS - 2, -1, -1):
        grp = jnp.where(gl[k] == gmax, k, grp)
    pg = 1.0 / functools.reduce(lambda a, b: a + b, [jnp.exp(x - gmax) for x in gl])
    el = []
    for k in range(EXP_PER_GROUP):
        sel = lt[N_GROUPS + (N_GROUPS - 1) * EXP_PER_GROUP + k:N_GROUPS + (N_GROUPS - 1) * EXP_PER_GROUP + k + 1, :]
        for gi in range(N_GROUPS - 2, -1, -1):
            r = N_GROUPS + gi * EXP_PER_GROUP + k
            sel = jnp.where(grp == gi, lt[r:r + 1, :], sel)
        el.append(sel)
    v1 = functools.reduce(jnp.maximum, el)
    i1 = jnp.full(v1.shape, EXP_PER_GROUP - 1, jnp.int32)
    for k in range(EXP_PER_GROUP - 2, -1, -1):
        i1 = jnp.where(el[k] == v1, k, i1)
    el2 = [jnp.where(i1 == k, -jnp.inf, el[k]) for k in range(EXP_PER_GROUP)]
    v2 = functools.reduce(jnp.maximum, el2)
    i2 = jnp.full(v2.shape, EXP_PER_GROUP - 1, jnp.int32)
    for k in range(EXP_PER_GROUP - 2, -1, -1):
        i2 = jnp.where(el2[k] == v2, k, i2)
    e21 = jnp.exp(v2 - v1)
    den = 1.0 / (1.0 + e21)
    c1 = pg * den
    c2 = pg * (e21 * den)
    e1 = grp * EXP_PER_GROUP + i1
    e2 = grp * EXP_PER_GROUP + i2
    eid = lax.broadcasted_iota(jnp.int32, (NE, tm), 0)
    oh1 = (eid == e1).astype(F32)
    oh2 = (eid == e2).astype(F32)
    oh = oh1 + oh2
    tr = lax.broadcasted_iota(jnp.int32, (tm, tm), 0)
    tc = lax.broadcasted_iota(jnp.int32, (tm, tm), 1)
    before = jnp.where(tr < tc, 1.0, 0.0).astype(BF16)
    base = carry_ref[...] + _dot(oh.astype(BF16), before)
    r1 = jnp.sum(oh1 * base, axis=0, keepdims=True)
    r2 = jnp.sum(oh2 * base, axis=0, keepdims=True)
    total = carry_ref[...] + jnp.sum(oh, axis=1, keepdims=True)
    carry_ref[...] = total
    cnt_ref[...] = jnp.broadcast_to(total, cnt_ref.shape)
    route_ref[...] = jnp.concatenate(
        [e1.astype(F32), e2.astype(F32), c1, c2, r1, r2, jnp.zeros((2, tm), F32)], axis=0)


def _router(y, g, w_gr, b_gr, w_er, b_er, *, tm):
    M, D = y.shape
    NE = N_GROUPS * EXP_PER_GROUP
    rows = 32
    wr = jnp.zeros((rows, D), F32).at[:N_GROUPS].set(w_gr.T).at[N_GROUPS:N_GROUPS + NE].set(w_er.T)
    br = jnp.zeros((rows, 1), F32).at[:N_GROUPS, 0].set(b_gr).at[N_GROUPS:N_GROUPS + NE, 0].set(b_er)
    return pl.pallas_call(
        _router_kernel,
        grid=(M // tm,),
        in_specs=[
            pl.BlockSpec((tm, D), lambda i: (i, 0)),
            pl.BlockSpec((1, D), lambda i: (0, 0)),
            pl.BlockSpec((rows, D), lambda i: (0, 0)),
            pl.BlockSpec((rows, 1), lambda i: (0, 0)),
        ],
        out_specs=[
            pl.BlockSpec((8, tm), lambda i: (0, i)),
            pl.BlockSpec((NE, 128), lambda i: (0, 0)),
        ],
        out_shape=[
            jax.ShapeDtypeStruct((8, M), F32),
            jax.ShapeDtypeStruct((NE, 128), F32),
        ],
        scratch_shapes=[pltpu.VMEM((NE, 1), F32)],
        compiler_params=_params(("arbitrary",)),
        name="moe_router",
    )(y, g.reshape(1, D), wr, br)


def _cast_kernel(w1_ref, w3_ref, w2_ref, o1_ref, o3_ref, o2_ref):
    o1_ref[0] = w1_ref[0, 0].astype(BF16)
    o3_ref[0] = w3_ref[0, 0].astype(BF16)
    o2_ref[0] = w2_ref[0, 0].astype(BF16)


def _cast_expert_weights(w1, w3, w2, *, layer):
    _, NE, D, FF = w1.shape
    nc = 2
    r1, r2 = D // nc, FF // nc
    return pl.pallas_call(
        _cast_kernel,
        grid=(NE, nc),
        in_specs=[
            pl.BlockSpec((1, 1, r1, FF), lambda e, c: (layer, e, c, 0)),
            pl.BlockSpec((1, 1, r1, FF), lambda e, c: (layer, e, c, 0)),
            pl.BlockSpec((1, 1, r2, D), lambda e, c: (layer, e, c, 0)),
        ],
        out_specs=[
            pl.BlockSpec((1, r1, FF), lambda e, c: (e, c, 0)),
            pl.BlockSpec((1, r1, FF), lambda e, c: (e, c, 0)),
            pl.BlockSpec((1, r2, D), lambda e, c: (e, c, 0)),
        ],
        out_shape=[
            jax.ShapeDtypeStruct((NE, D, FF), BF16),
            jax.ShapeDtypeStruct((NE, D, FF), BF16),
            jax.ShapeDtypeStruct((NE, FF, D), BF16),
        ],
        compiler_params=_params(("parallel", "parallel")),
        name="moe_cast_weights",
    )(w1, w3, w2)


def _ffn_kernel(te_ref, nu_ref, x_ref, g_ref, w1_ref, w3_ref, w2_ref, o_ref, xb_ref):
    del te_ref
    i = pl.program_id(0)
    f = pl.program_id(1)
    used = i < nu_ref[0]

    @pl.when(jnp.logical_and(used, f == 0))
    def _():
        x = x_ref[...]
        xb_ref[...] = (x * _rms_scale(x, x.shape[-1]) * g_ref[...]).astype(BF16)

    @pl.when(used)
    def _():
        x = xb_ref[...]
        a = _dot(x, w1_ref[0])
        b = _dot(x, w3_ref[0])
        hh = (a * jax.nn.sigmoid(a)) * b
        y = _dot(hh.astype(BF16), w2_ref[0])

        @pl.when(f == 0)
        def _():
            o_ref[...] = y

        @pl.when(f > 0)
        def _():
            o_ref[...] += y

    @pl.when(jnp.logical_and(jnp.logical_not(used), f == 0))
    def _():
        o_ref[...] = jnp.zeros_like(o_ref)


def _expert_ffn(xs, g, tile_expert, n_used, w1, w3, w2, *, tm, tf):
    NS, D = xs.shape
    FF = w1.shape[2]
    nT = NS // tm

    def xi(i, f, te, nu):
        return (jnp.minimum(i, nu[0] - 1), 0)

    return pl.pallas_call(
        _ffn_kernel,
        grid_spec=pltpu.PrefetchScalarGridSpec(
            num_scalar_prefetch=2,
            grid=(nT, FF // tf),
            in_specs=[
                pl.BlockSpec((tm, D), xi),
                pl.BlockSpec((1, D), lambda i, f, te, nu: (0, 0)),
                pl.BlockSpec((1, D, tf), lambda i, f, te, nu: (te[i], 0, f)),
                pl.BlockSpec((1, D, tf), lambda i, f, te, nu: (te[i], 0, f)),
                pl.BlockSpec((1, tf, D), lambda i, f, te, nu: (te[i], f, 0)),
            ],
            out_specs=pl.BlockSpec((tm, D), lambda i, f, te, nu: (i, 0)),
            scratch_shapes=[pltpu.VMEM((tm, D), BF16)],
        ),
        out_shape=jax.ShapeDtypeStruct((NS, D), F32),
        compiler_params=_params(("arbitrary", "arbitrary")),
        name="moe_ffn",
    )(tile_expert, n_used, xs, g.reshape(1, D), w1, w3, w2)


def _combine_kernel(y_ref, a_ref, b_ref, c_ref, o_ref):
    c = c_ref[...]
    o_ref[...] = y_ref[...] + (c[:, 0:1] * a_ref[...] + c[:, 1:2] * b_ref[...])


def _combine(y, a, b, c, *, tm, row0=0, rows=None):
    M, D = y.shape
    rows = M - row0 if rows is None else rows
    assert row0 % tm == 0 and rows % tm == 0
    rb0 = row0 // tm
    row = pl.BlockSpec((tm, D), lambda i: (rb0 + i, 0))
    return pl.pallas_call(
        _combine_kernel,
        grid=(rows // tm,),
        in_specs=[row, row, row, pl.BlockSpec((tm, 2), lambda i: (rb0 + i, 0))],
        out_specs=pl.BlockSpec((tm, D), lambda i: (i, 0)),
        out_shape=jax.ShapeDtypeStruct((rows, D), F32),
        compiler_params=_params(("parallel",)),
        name="moe_combine",
    )(y, a, b, c)


def _moe(y, g, w_gr, b_gr, w_er, b_er, w1, w3, w2, *, layer, tm, split=None):
    M, D = y.shape
    NE = N_GROUPS * EXP_PER_GROUP
    route, cnt = _router(y, g, w_gr, b_gr, w_er, b_er, tm=tm)
    e1 = route[0].astype(jnp.int32)
    e2 = route[1].astype(jnp.int32)
    r1 = route[4].astype(jnp.int32)
    r2 = route[5].astype(jnp.int32)
    counts = cnt[:, 0].astype(jnp.int32)
    padded = ((counts + tm - 1) // tm) * tm
    ends = jnp.cumsum(padded)
    off = ends - padded
    d1 = off[e1] + r1
    d2 = off[e2] + r2
    NS = ((2 * M + NE * (tm - 1)) // tm) * tm
    tok = jnp.arange(M, dtype=jnp.int32)
    src = (jnp.arange(NS, dtype=jnp.int32) % M).at[jnp.concatenate([d1, d2])].set(
        jnp.concatenate([tok, tok]), unique_indices=True, mode="promise_in_bounds")
    n_used = (ends[-1] // tm).astype(jnp.int32).reshape(1)
    tile_start = jnp.minimum(jnp.arange(NS // tm, dtype=jnp.int32) * tm, ends[-1] - tm)
    tile_expert = jnp.sum((ends[None, :] <= tile_start[:, None]).astype(jnp.int32), axis=1)
    tile_expert = jnp.minimum(tile_expert, NE - 1)
    xs = y.at[src].get(mode="promise_in_bounds")
    w1b, w3b, w2b = _cast_expert_weights(w1, w3, w2, layer=layer)
    tf = _tile(w1.shape[3], 512)
    ys = _expert_ffn(xs, g, tile_expert, n_used, w1b, w3b, w2b, tm=tm, tf=tf)
    a = ys.at[d1].get(mode="promise_in_bounds")
    b = ys.at[d2].get(mode="promise_in_bounds")
    c = route[2:4].T
    if split is None:
        return _combine(y, a, b, c, tm=tm)
    return _combine(y, a, b, c, tm=tm, rows=split), _combine(y, a, b, c, tm=tm, row0=split)


def _latent_kernel(x_ref, gs_ref, w_ref, gc_ref, ckv_ref, kpe_ref, kpew_ref):
    x = x_ref[...]
    s = x * _rms_scale(x, x.shape[-1]) * gs_ref[...]
    ck = _dot(s.astype(BF16), w_ref[...])
    lora = ckv_ref.shape[1]
    wide = kpew_ref.shape[1]
    c = ck[:, :lora]
    ckv_ref[...] = c * _rms_scale(c, lora) * gc_ref[...]
    kpew_ref[...] = ck[:, lora:lora + wide]
    kpe_ref[...] = ck[:, lora + wide:]


def _latent(y, g_src, w_dkv, g_ckv, *, tm):
    M, D = y.shape
    lora = g_ckv.shape[0]
    w_pe = w_dkv[:, lora:]
    w_all = jnp.concatenate([w_dkv[:, :lora], _spread(w_pe), w_pe], axis=1).astype(BF16)
    N = w_all.shape[1]
    return pl.pallas_call(
        _latent_kernel,
        grid=(M // tm,),
        in_specs=[
            pl.BlockSpec((tm, D), lambda i: (i, 0)),
            pl.BlockSpec((1, D), lambda i: (0, 0)),
            pl.BlockSpec((D, N), lambda i: (0, 0)),
            pl.BlockSpec((1, lora), lambda i: (0, 0)),
        ],
        out_specs=[
            pl.BlockSpec((tm, lora), lambda i: (i, 0)),
            pl.BlockSpec((tm, ROPE_DIM), lambda i: (i, 0)),
            pl.BlockSpec((tm, 2 * ROPE_DIM), lambda i: (i, 0)),
        ],
        out_shape=[
            jax.ShapeDtypeStruct((M, lora), F32),
            jax.ShapeDtypeStruct((M, ROPE_DIM), F32),
            jax.ShapeDtypeStruct((M, 2 * ROPE_DIM), F32),
        ],
        compiler_params=_params(("parallel",)),
        name="shared_latent",
    )(y, g_src.reshape(1, D), w_all, g_ckv.reshape(1, lora))


def _spread(x):
    half = ROPE_DIM // 2
    z = jnp.zeros(x.shape[:-1] + (half,), x.dtype)
    return jnp.concatenate([x[..., :half], z, x[..., half:], z], axis=-1)


def _head_norm_rope(nope, pe, gain, cw, sw):
    ssq = jnp.sum(nope * nope + pe * pe, axis=-1, keepdims=True)
    inv = lax.rsqrt(ssq * (1.0 / QK_DIM) + EPS)
    n = nope * inv * gain[:, :NOPE_DIM]
    p = pe * inv * gain[:, NOPE_DIM:]
    p = p * cw + pltpu.roll(p, ROPE_DIM, axis=1) * sw
    return jnp.concatenate([n, p], axis=-1)


def _kv_prep_kernel(ckv_ref, kpe_ref, wuk_ref, wuv_ref, gk_ref, cos_ref, sin_ref, k_ref, v_ref):
    cb = ckv_ref[...].astype(BF16)
    kn = _dot(cb, wuk_ref[...])
    v_ref[...] = _dot(cb, wuv_ref[...]).astype(v_ref.dtype)
    pe, gain, cw, sw = kpe_ref[...], gk_ref[...], cos_ref[...], sin_ref[...]
    for hh in range(k_ref.shape[0]):
        k_ref[hh] = _head_norm_rope(kn[:, hh * NOPE_DIM:(hh + 1) * NOPE_DIM], pe, gain, cw, sw).astype(k_ref.dtype)


def _kv_prep(ckv, kpe_wide, w_uk, w_uv, g_wide, cw, sw, *, rows, H, tm):
    lora = ckv.shape[1]
    HB = math.gcd(H, 4)
    return pl.pallas_call(
        _kv_prep_kernel,
        grid=(rows // tm, H // HB),
        in_specs=[
            pl.BlockSpec((tm, lora), lambda i, h: (i, 0)),
            pl.BlockSpec((tm, 2 * ROPE_DIM), lambda i, h: (i, 0)),
            pl.BlockSpec((lora, HB * NOPE_DIM), lambda i, h: (0, h)),
            pl.BlockSpec((lora, HB * V_DIM), lambda i, h: (0, h)),
            pl.BlockSpec((1, QK_PAD), lambda i, h: (0, 0)),
            pl.BlockSpec((tm, 2 * ROPE_DIM), lambda i, h: (i, 0)),
            pl.BlockSpec((tm, 2 * ROPE_DIM), lambda i, h: (i, 0)),
        ],
        out_specs=[
            pl.BlockSpec((HB, tm, QK_PAD), lambda i, h: (h, i, 0)),
            pl.BlockSpec((tm, HB * V_DIM), lambda i, h: (i, h)),
        ],
        out_shape=[
            jax.ShapeDtypeStruct((H, rows, QK_PAD), BF16),
            jax.ShapeDtypeStruct((rows, H * V_DIM), BF16),
        ],
        compiler_params=_params(("parallel", "arbitrary")),
        name="mla_kv_prep",
    )(ckv, kpe_wide, w_uk, w_uv, g_wide.reshape(1, QK_PAD), cw, sw)


def _q_prep_kernel(q_ref, gq_ref, cos_ref, sin_ref, o_ref):
    gain, cw, sw = gq_ref[...], cos_ref[...], sin_ref[...]
    for hh in range(o_ref.shape[0]):
        q = q_ref[:, hh * QK_PAD:(hh + 1) * QK_PAD]
        o_ref[hh] = _head_norm_rope(q[:, :NOPE_DIM], q[:, NOPE_DIM:], gain, cw, sw).astype(o_ref.dtype)


def _q_prep(qraw, g_wide, cw, sw, *, H, tm):
    M = qraw.shape[0]
    HB = math.gcd(H, 4)
    return pl.pallas_call(
        _q_prep_kernel,
        grid=(M // tm, H // HB),
        in_specs=[
            pl.BlockSpec((tm, HB * QK_PAD), lambda i, h: (i, h)),
            pl.BlockSpec((1, QK_PAD), lambda i, h: (0, 0)),
            pl.BlockSpec((tm, 2 * ROPE_DIM), lambda i, h: (i, 0)),
            pl.BlockSpec((tm, 2 * ROPE_DIM), lambda i, h: (i, 0)),
        ],
        out_specs=pl.BlockSpec((HB, tm, QK_PAD), lambda i, h: (h, i, 0)),
        out_shape=jax.ShapeDtypeStruct((H, M, QK_PAD), BF16),
        compiler_params=_params(("parallel", "arbitrary")),
        name="mla_q_prep",
    )(qraw, g_wide.reshape(1, QK_PAD), cw, sw)


def _flash_kernel(q_ref, k_ref, v_ref, o_ref, m_ref, acc_ref):
    qi = pl.program_id(2)
    HB, tq, _ = q_ref.shape
    scale = QK_DIM ** -0.5
    m_ref[...] = jnp.full_like(m_ref, -jnp.inf)
    acc_ref[...] = jnp.zeros_like(acc_ref)
    ones = jnp.ones((tq, V_DIM), BF16)

    def step(start, masked):
        for h in range(HB):
            k = k_ref[h, pl.ds(start, tq), :]
            v = jnp.concatenate([v_ref[pl.ds(start, tq), h * V_DIM:(h + 1) * V_DIM], ones], axis=1)
            s = _dot_nt(q_ref[h], k) * scale
            if masked:
                row = lax.broadcasted_iota(jnp.int32, (tq, tq), 0)
                col = lax.broadcasted_iota(jnp.int32, (tq, tq), 1)
                s = jnp.where(col <= row, s, -jnp.inf)
            m_old = m_ref[h]
            m_new = jnp.maximum(m_old, jnp.max(s, axis=-1, keepdims=True))
            p = jnp.exp(s - m_new)
            acc_ref[h] = jnp.exp(m_old - m_new) * acc_ref[h] + _dot(p.astype(BF16), v)
            m_ref[h] = m_new

    def full_tile(ki, carry):
        step(pl.multiple_of(ki * tq, tq), False)
        return carry

    lax.fori_loop(0, qi, full_tile, 0)
    step(pl.multiple_of(qi * tq, tq), True)
    for h in range(HB):
        acc = acc_ref[h]
        o_ref[:, h * V_DIM:(h + 1) * V_DIM] = acc[:, :V_DIM] / acc[:, V_DIM:V_DIM + 1]


def _flash_prompt(qpad, kpad, v, *, B, T, H, M, tq):
    nq = T // tq
    HB = math.gcd(H, 4)
    return pl.pallas_call(
        _flash_kernel,
        grid=(B, H // HB, nq),
        in_specs=[
            pl.BlockSpec((HB, tq, QK_PAD), lambda b, h, qi: (h, b * nq + qi, 0)),
            pl.BlockSpec((HB, T, QK_PAD), lambda b, h, qi: (h, b, 0)),
            pl.BlockSpec((T, HB * V_DIM), lambda b, h, qi: (b, h)),
        ],
        out_specs=pl.BlockSpec((tq, HB * V_DIM), lambda b, h, qi: (b * nq + qi, h)),
        out_shape=jax.ShapeDtypeStruct((M, H * V_DIM), F32),
        scratch_shapes=[pltpu.VMEM((HB, tq, 1), F32), pltpu.VMEM((HB, tq, 2 * V_DIM), F32)],
        compiler_params=_params(("parallel", "parallel", "arbitrary")),
        name="mla_prompt_attention",
    )(qpad, kpad, v)


def _absorb_kernel(q_ref, wuk_ref, gk_ref, o_ref):
    qg = q_ref[0][:, :NOPE_DIM].astype(F32) * gk_ref[...]
    o_ref[0] = _dot_nt(qg.astype(BF16), wuk_ref[...]).astype(o_ref.dtype)


def _absorb(qpad, w_uk, g_k, *, H, Mp, Ms):
    lora = w_uk.shape[0]
    assert Mp % Ms == 0
    return pl.pallas_call(
        _absorb_kernel,
        grid=(H,),
        in_specs=[
            pl.BlockSpec((1, Ms, QK_PAD), lambda h: (h, Mp // Ms, 0)),
            pl.BlockSpec((lora, NOPE_DIM), lambda h: (0, h)),
            pl.BlockSpec((1, NOPE_DIM), lambda h: (0, 0)),
        ],
        out_specs=pl.BlockSpec((1, Ms, lora), lambda h: (h, 0, 0)),
        out_shape=jax.ShapeDtypeStruct((H, Ms, lora), BF16),
        compiler_params=_params(("parallel",)),
        name="mla_absorb_queries",
    )(qpad, w_uk, g_k[:NOPE_DIM].reshape(1, NOPE_DIM))


def _decode_kernel(pt_ref, qabs_ref, qpe_ref, wt_ref, *refs, T, H, P):
    del pt_ref
    ckv_pages = refs[:P]
    kpe_pages = refs[P:2 * P]
    (cnew_ref, knew_ref, cos_ref, sin_ref, cosn_ref, sinn_ref, g1_ref, g2_ref,
     o_ref, m_ref, l_ref, acc_ref) = refs[2 * P:]
    b = pl.program_id(0)
    t = pl.program_id(1)
    last = pl.num_programs(1) - 1
    page = ckv_pages[0].shape[1]
    lora = ckv_pages[0].shape[2]
    R = T * H
    NR = cnew_ref.shape[0]
    pps = SUB_POS // page

    @pl.when(t == 0)
    def _():
        m_ref[...] = jnp.full_like(m_ref, -jnp.inf)
        l_ref[...] = jnp.zeros_like(l_ref)
        acc_ref[...] = jnp.zeros_like(acc_ref)

    def scores(cb, kpe_t, cos, sin):
        W = cb.shape[0]
        kn = _dot_nt(wt_ref[...], cb)
        ssq = jnp.sum((kn * kn).reshape(H, NOPE_DIM, W), axis=1)
        sn = _dot_nt(qabs_ref[0], cb)
        ssq = ssq + jnp.sum(kpe_t * kpe_t, axis=0, keepdims=True)
        inv = lax.rsqrt(ssq * (1.0 / QK_DIM) + EPS)
        x1 = kpe_t[:ROPE_DIM // 2] * g1_ref[...]
        x2 = kpe_t[ROPE_DIM // 2:] * g2_ref[...]
        kr = jnp.concatenate([x1 * cos - x2 * sin, x1 * sin + x2 * cos], axis=0)
        sr = _dot(qpe_ref[0], kr.astype(BF16))
        return (sn + sr) * jnp.concatenate([inv] * T, axis=0) * (QK_DIM ** -0.5)

    def softmax_update(s, cb):
        m_old = m_ref[...]
        m_new = jnp.maximum(m_old, jnp.max(s, axis=-1, keepdims=True))
        alpha = jnp.exp(m_old - m_new)
        pr = jnp.exp(s - m_new)
        l_ref[...] = alpha * l_ref[...] + jnp.sum(pr, axis=-1, keepdims=True)
        acc_ref[...] = alpha * acc_ref[...] + _dot(pr.astype(BF16), cb)
        m_ref[...] = m_new

    cbs, ss = [], []
    for j in range(P // pps):
        pages = range(j * pps, (j + 1) * pps)
        cb = jnp.concatenate([ckv_pages[p][0].astype(BF16) for p in pages], axis=0)
        kpe_t = jnp.concatenate([kpe_pages[p][0] for p in pages], axis=1)
        cbs.append(cb)
        ss.append(scores(cb, kpe_t, cos_ref[:, j * SUB_POS:(j + 1) * SUB_POS],
                         sin_ref[:, j * SUB_POS:(j + 1) * SUB_POS]))
    softmax_update(jnp.concatenate(ss, axis=1), jnp.concatenate(cbs, axis=0))

    @pl.when(t == last)
    def _():
        cb = jnp.concatenate([cnew_ref[...], jnp.zeros((page - NR, lora), F32)], axis=0).astype(BF16)
        q_tok = lax.broadcasted_iota(jnp.int32, (R, page), 0) // H
        lane = lax.broadcasted_iota(jnp.int32, (R, page), 1)
        valid = jnp.logical_and(lane // T == b % (NR // T), lane % T <= q_tok)
        valid = jnp.logical_and(valid, lane < NR)
        s = scores(cb, knew_ref[0], cosn_ref[...], sinn_ref[...])
        softmax_update(jnp.where(valid, s, -jnp.inf), cb)
        o_ref[0] = acc_ref[...] / l_ref[...]


def _decode_attention(qabs, qpe, w_uk_t, cache_ckv, cache_kpe_t, page_table, ckv_all, kpe_new_t,
                      cos_t, sin_t, g_k, *, Bd, T, H, Mp):
    n_pages = page_table.shape[1]
    P = math.gcd(n_pages, PAGES_PER_STEP)
    page = cache_ckv.shape[1]
    lora = cache_ckv.shape[2]
    assert n_pages % P == 0 and SUB_POS % page == 0 and (P * page) % SUB_POS == 0
    steps = n_pages // P
    R = T * H
    NR = 8
    assert NR % T == 0 and Mp % NR == 0
    spb = NR // T
    pt = page_table.reshape(-1).astype(jnp.int32)
    half = ROPE_DIM // 2
    g1 = g_k[NOPE_DIM:NOPE_DIM + half].reshape(half, 1)
    g2 = g_k[NOPE_DIM + half:].reshape(half, 1)

    def page_spec(p, shape):
        def idx(b, t, pt):
            return (pt[b * n_pages + t * P + p], 0, 0)
        return pl.BlockSpec((1,) + shape, idx)

    in_specs = [
        pl.BlockSpec((1, R, lora), lambda b, t, pt: (b, 0, 0)),
        pl.BlockSpec((1, R, ROPE_DIM), lambda b, t, pt: (b, 0, 0)),
        pl.BlockSpec(w_uk_t.shape, lambda b, t, pt: (0, 0)),
    ]
    in_specs += [page_spec(p, (page, lora)) for p in range(P)]
    in_specs += [page_spec(p, (ROPE_DIM, page)) for p in range(P)]
    in_specs += [
        pl.BlockSpec((NR, lora), lambda b, t, pt: (Mp // NR + b // spb, 0)),
        pl.BlockSpec((1, ROPE_DIM, page), lambda b, t, pt: (b // spb, 0, 0)),
        pl.BlockSpec((half, P * page), lambda b, t, pt: (0, t)),
        pl.BlockSpec((half, P * page), lambda b, t, pt: (0, t)),
        pl.BlockSpec((half, page), lambda b, t, pt: (0, n_pages)),
        pl.BlockSpec((half, page), lambda b, t, pt: (0, n_pages)),
        pl.BlockSpec((half, 1), lambda b, t, pt: (0, 0)),
        pl.BlockSpec((half, 1), lambda b, t, pt: (0, 0)),
    ]
    return pl.pallas_call(
        functools.partial(_decode_kernel, T=T, H=H, P=P),
        grid_spec=pltpu.PrefetchScalarGridSpec(
            num_scalar_prefetch=1,
            grid=(Bd, steps),
            in_specs=in_specs,
            out_specs=pl.BlockSpec((1, R, lora), lambda b, t, pt: (b, 0, 0)),
            scratch_shapes=[
                pltpu.VMEM((R, 1), F32),
                pltpu.VMEM((R, 1), F32),
                pltpu.VMEM((R, lora), F32),
            ],
        ),
        out_shape=jax.ShapeDtypeStruct((Bd, R, lora), F32),
        compiler_params=_params(("parallel", "arbitrary")),
        name="mla_decode_attention",
    )(pt, qabs, qpe, w_uk_t, *([cache_ckv] * P), *([cache_kpe_t] * P), ckv_all, kpe_new_t, cos_t, sin_t, cos_t, sin_t,
      g1, g2)


def _head_values_kernel(c_ref, w_ref, alias_ref, o_ref):
    del alias_ref
    o_ref[...] = _dot(c_ref[...].astype(BF16), w_ref[...])


def _head_values(ctx, w_uv, attn, *, H, Mp, Ms):
    lora = w_uv.shape[0]
    return pl.pallas_call(
        _head_values_kernel,
        grid=(H,),
        in_specs=[
            pl.BlockSpec((Ms, lora), lambda h: (0, h)),
            pl.BlockSpec((lora, V_DIM), lambda h: (0, h)),
            pl.BlockSpec(memory_space=pl.ANY),
        ],
        out_specs=pl.BlockSpec((Ms, V_DIM), lambda h: (Mp // Ms, h)),
        out_shape=jax.ShapeDtypeStruct(attn.shape, F32),
        input_output_aliases={2: 0},
        compiler_params=_params(("parallel",)),
        name="mla_decode_values",
    )(ctx, w_uv, attn)


def _rope_tables(pos, d):
    freq = ROPE_THETA ** (-jnp.arange(0, d, 2, dtype=F32) / d)
    ang = pos.astype(F32)[:, None] * freq[None, :]
    return jnp.cos(ang), jnp.sin(ang)


def kernel(x_prompt, x_sample, state_ret, cache_ckv, cache_kpe, page_table, ln_mix, ln_ffn, ret_w_in, ret_w_out,
           kv_src_norm, w_dkv, ckv_norm, w_ukv, k_norm, mla_w_q, q_norm, mla_w_o, moe_w_gr, moe_b_gr, moe_w_er,
           moe_b_er, moe_w1, moe_w3, moe_w2):
    B, T, D = x_prompt.shape
    Bd, Td, _ = x_sample.shape
    n_a = state_ret.shape[0]
    depth = ln_mix.shape[0]
    RH, dk, dv = state_ret.shape[2:]
    lora, MH = w_ukv.shape[0], w_ukv.shape[1]
    Mp, Ms = B * T, Bd * Td
    M = Mp + Ms
    past = page_table.shape[1] * cache_ckv.shape[1]
    tm = _tile(math.gcd(Mp, Ms), 512)
    tm_p = _tile(Mp, 1024)

    pos_p = jnp.arange(T)
    pos_s = past + jnp.arange(Td)
    log_g = jnp.log1p(-jnp.exp2(-5.0 - jnp.arange(RH, dtype=F32)))
    y = jnp.concatenate([x_prompt.reshape(Mp, D), x_sample.reshape(Ms, D)], axis=0)

    ret_p, ret_s = [], []
    ckv = kpe = kpad = vals = None
    for i in range(depth):
        if i == n_a:
            ckv, kpe, kpe_wide = _latent(y, kv_src_norm, w_dkv, ckv_norm, tm=tm)
            pos_all = jnp.concatenate([jnp.tile(pos_p, B), jnp.tile(pos_s, Bd)])
            cos_m, sin_m = _rope_tables(pos_all, ROPE_DIM)
            cw = _spread(jnp.concatenate([cos_m, cos_m], axis=1))
            sw = _spread(jnp.concatenate([-sin_m, sin_m], axis=1))
            w_uk = w_ukv[..., :NOPE_DIM].reshape(lora, MH * NOPE_DIM).astype(BF16)
            w_uv = w_ukv[..., NOPE_DIM:].reshape(lora, MH * V_DIM).astype(BF16)
            gk_wide = jnp.concatenate([k_norm[:NOPE_DIM], _spread(k_norm[NOPE_DIM:])])
            kpad, vals = _kv_prep(ckv, kpe_wide, w_uk, w_uv, gk_wide, cw, sw, rows=Mp, H=MH, tm=tm)
        if i < n_a:
            w_in = ret_w_in[i].astype(BF16)
            qkvg = _matmul_streams(y, w_in, g=ln_mix[i], Mp=Mp, tm_p=tm_p, tm_s=tm,
                                   tn=_tile(w_in.shape[1], 1024), name="ret_in_proj")
            cos_p, sin_p = _rope_tables(pos_p, dk)
            gated, s_p = _retention_prompt(qkvg, log_g, cos_p, sin_p, B=B, T=T, H=RH, dk=dk, dv=dv, M=M)
            cos_s, sin_s = _rope_tables(jnp.tile(pos_s, 8 // Td), dk)
            gated, s_s = _retention_sample(qkvg, log_g, cos_s, sin_s, state_ret[i], gated,
                                           Bd=Bd, T=Td, H=RH, dk=dk, dv=dv, Mp=Mp)
            ret_p.append(s_p)
            ret_s.append(s_s)
            y = _matmul(gated, ret_w_out[i].astype(BF16), res=y, tm=tm, tn=_tile(D, 512), name="ret_out_proj")
        else:
            j = i - n_a
            wq = mla_w_q[j].reshape(D, MH, QK_DIM)
            wq = jnp.concatenate([wq[..., :NOPE_DIM], _spread(wq[..., NOPE_DIM:])], axis=-1)
            wq = wq.reshape(D, MH * QK_PAD).astype(BF16)
            qraw = _matmul_streams(y, wq, g=ln_mix[i], Mp=Mp, tm_p=tm_p, tm_s=tm,
                                   tn=_tile(MH * QK_PAD, 1024), name="mla_q_proj")
            gq_wide = jnp.concatenate([q_norm[j][:NOPE_DIM], _spread(q_norm[j][NOPE_DIM:])])
            qpad = _q_prep(qraw, gq_wide, cw, sw, H=MH, tm=tm)
            attn = _flash_prompt(qpad, kpad, vals, B=B, T=T, H=MH, M=M, tq=_tile(T, 512))
            qabs = _absorb(qpad, w_uk, k_norm, H=MH, Mp=Mp, Ms=Ms)
            qabs = qabs.reshape(MH, Bd, Td, lora).transpose(1, 2, 0, 3).reshape(Bd, Td * MH, lora)
            half = ROPE_DIM // 2
            qpe = jnp.concatenate([qpad[:, Mp:, NOPE_DIM:NOPE_DIM + half],
                                   qpad[:, Mp:, NOPE_DIM + ROPE_DIM:NOPE_DIM + ROPE_DIM + half]], axis=-1)
            qpe = qpe.reshape(MH, Bd, Td, ROPE_DIM).transpose(1, 2, 0, 3).reshape(Bd, Td * MH, ROPE_DIM)
            page = cache_ckv.shape[1]
            col = jnp.arange(past + page)
            pos_cols = jnp.where(col < past, col, past + (col - past) % Td)
            cos_c, sin_c = _rope_tables(pos_cols, ROPE_DIM)
            w_uk_t = w_uk.T
            spb = 8 // Td
            kpe_new_t = kpe[Mp:].reshape(Bd // spb, spb * Td, ROPE_DIM).swapaxes(1, 2)
            kpe_new_t = jnp.pad(kpe_new_t, ((0, 0), (0, 0), (0, page - spb * Td)))
            ctx = _decode_attention(qabs, qpe, w_uk_t, cache_ckv, cache_kpe.swapaxes(1, 2), page_table, ckv,
                                    kpe_new_t, cos_c.T, sin_c.T, k_norm, Bd=Bd, T=Td, H=MH, Mp=Mp)
            attn = _head_values(ctx.reshape(Ms, MH * lora), w_uv, attn, H=MH, Mp=Mp, Ms=Ms)
            y = _matmul_streams(attn, mla_w_o[j].astype(BF16), res=y, Mp=Mp, tm_p=tm_p, tm_s=tm,
                                tn=_tile(D, 512), name="mla_out_proj")
        y = _moe(y, ln_ffn[i], moe_w_gr[i], moe_b_gr[i], moe_w_er[i], moe_b_er[i],
                 moe_w1, moe_w3, moe_w2, layer=i, tm=tm, split=Mp if i == depth - 1 else None)

    yp = y[0].reshape(B, T, D)
    ys = y[1].reshape(Bd, Td, D)
    return (yp, ys, jnp.stack(ret_p, axis=0), jnp.stack(ret_s, axis=0),
            ckv[:Mp].reshape(B, T, lora), kpe[:Mp].reshape(B, T, ROPE_DIM),
            ckv[Mp:].reshape(Bd, Td, lora), kpe[Mp:].reshape(Bd, Td, ROPE_DIM))
```

```python
import functools
import math

import jax
import jax.numpy as jnp
from jax import lax
from jax.experimental import pallas as pl
from jax.experimental.pallas import tpu as pltpu

F32 = jnp.float32
BF16 = jnp.bfloat16
EPS = 1e-6
ROPE_THETA = 10000.0
RET_CHUNK = 128
NOPE_DIM = 128
ROPE_DIM = 64
V_DIM = 128
QK_DIM = NOPE_DIM + ROPE_DIM
QK_PAD = 256
N_GROUPS = 4
EXP_PER_GROUP = 4
PAGES_PER_STEP = 32
SUB_POS = 512
MIB = 1024 * 1024


def _params(sem, vmem_mib=48):
    return pltpu.CompilerParams(dimension_semantics=sem, vmem_limit_bytes=vmem_mib * MIB)


def _tile(n, pref):
    t = math.gcd(n, pref)
    assert t % 8 == 0 or t == n, (n, pref)
    return t


def _dot(a, b):
    return jnp.dot(a, b, preferred_element_type=F32)


def _dot_nt(a, b):
    return lax.dot_general(a, b, (((1,), (1,)), ((), ())), preferred_element_type=F32)


def _rms_scale(x, n):
    return lax.rsqrt(jnp.sum(x * x, axis=-1, keepdims=True) * (1.0 / n) + EPS)


def _rope_half(x, cos, sin):
    half = x.shape[-1] // 2
    x1, x2 = x[:, :half], x[:, half:]
    return jnp.concatenate([x1 * cos - x2 * sin, x1 * sin + x2 * cos], axis=-1)


def _mm_kernel(*refs, has_norm, has_res, has_into):
    it = iter(refs)
    x_ref = next(it)
    g_ref = next(it) if has_norm else None
    w_ref = next(it)
    r_ref = next(it) if has_res else None
    if has_into:
        next(it)
    o_ref = next(it)
    xb_ref = next(it)

    @pl.when(pl.program_id(1) == 0)
    def _():
        x = x_ref[...].astype(F32)
        if has_norm:
            x = x * _rms_scale(x, x.shape[-1]) * g_ref[...]
        xb_ref[...] = x.astype(BF16)

    acc = _dot(xb_ref[...], w_ref[...])
    if has_res:
        acc = r_ref[...] + acc
    o_ref[...] = acc.astype(o_ref.dtype)


def _matmul(x, w, *, g=None, res=None, row0=0, rows=None, into=None, tm, tn, name):
    M, K = x.shape
    N = w.shape[1]
    rows = M - row0 if rows is None else rows
    assert row0 % tm == 0 and rows % tm == 0
    rb0 = row0 // tm
    in_specs = [pl.BlockSpec((tm, K), lambda i, j: (rb0 + i, 0))]
    args = [x]
    if g is not None:
        in_specs.append(pl.BlockSpec((1, K), lambda i, j: (0, 0)))
        args.append(g.reshape(1, K))
    in_specs.append(pl.BlockSpec((K, tn), lambda i, j: (0, j)))
    args.append(w)
    if res is not None:
        in_specs.append(pl.BlockSpec((tm, tn), lambda i, j: (rb0 + i, j)))
        args.append(res)
    aliases = {}
    if into is not None:
        aliases = {len(args): 0}
        in_specs.append(pl.BlockSpec(memory_space=pl.ANY))
        args.append(into)
    return pl.pallas_call(
        functools.partial(_mm_kernel, has_norm=g is not None, has_res=res is not None, has_into=into is not None),
        grid=(rows // tm, N // tn),
        in_specs=in_specs,
        out_specs=pl.BlockSpec((tm, tn), lambda i, j: (rb0 + i, j)),
        out_shape=jax.ShapeDtypeStruct((M, N), F32),
        input_output_aliases=aliases,
        scratch_shapes=[pltpu.VMEM((tm, K), BF16)],
        compiler_params=_params(("parallel", "arbitrary")),
        name=name,
    )(*args)


def _matmul_streams(x, w, *, Mp, tm_p, tm_s, **kw):
    name = kw.pop("name")
    out = _matmul(x, w, rows=Mp, tm=tm_p, name=name + "_prompt", **kw)
    return _matmul(x, w, row0=Mp, into=out, tm=tm_s, name=name + "_decode", **kw)


def _ret_prompt_kernel(lg_ref, q_ref, k_ref, v_ref, g_ref, cos_ref, sin_ref, o_ref, s_out_ref, s_ref):
    hg = pl.program_id(1)
    c = pl.program_id(2)
    L = q_ref.shape[0]
    HB, dk, dv = s_ref.shape

    @pl.when(c == 0)
    def _():
        s_ref[...] = jnp.zeros_like(s_ref)

    cos, sin = cos_ref[...], sin_ref[...]
    ii = lax.broadcasted_iota(jnp.int32, (L, L), 0)
    jj = lax.broadcasted_iota(jnp.int32, (L, L), 1)
    rel = (ii - jj).astype(F32)
    ri = lax.broadcasted_iota(jnp.int32, (L, 1), 0).astype(F32)
    for hh in range(HB):
        lg = lg_ref[hg * HB + hh]
        q = _rope_half(q_ref[:, hh * dk:(hh + 1) * dk], cos, sin)
        k = _rope_half(k_ref[:, hh * dk:(hh + 1) * dk], cos, sin) * (dk ** -0.5)
        vb = v_ref[:, hh * dv:(hh + 1) * dv].astype(BF16)
        qb = q.astype(BF16)
        decay = jnp.where(rel >= 0, jnp.exp(lg * jnp.maximum(rel, 0.0)), 0.0)
        scores = _dot_nt(qb, k.astype(BF16)) * decay
        inner = _dot(scores.astype(BF16), vb)
        s_old = s_ref[hh]
        cross = _dot(qb, s_old.astype(BF16)) * jnp.exp((ri + 1.0) * lg)
        k_dec = k * jnp.exp((L - 1.0 - ri) * lg)
        g_all = jnp.exp(jnp.zeros((1, 1), F32) + L * lg)
        s_new = g_all * s_old + _dot(k_dec.T.astype(BF16), vb)
        s_ref[hh] = s_new
        o = inner + cross
        o = o * _rms_scale(o, dv)
        gate = g_ref[:, hh * dv:(hh + 1) * dv]
        o_ref[:, hh * dv:(hh + 1) * dv] = (gate * jax.nn.sigmoid(gate)) * o

    @pl.when(c == pl.num_programs(2) - 1)
    def _():
        s_out_ref[0] = s_ref[...]


def _retention_prompt(qkvg, log_g, cos, sin, *, B, T, H, dk, dv, M):
    C = RET_CHUNK if T % RET_CHUNK == 0 else T
    nC = T // C
    HB = math.gcd(H, 4)
    assert (2 * H * dk) % (HB * dv) == 0
    nH = H // HB
    voff = 2 * H * dk // (HB * dv)
    return pl.pallas_call(
        _ret_prompt_kernel,
        grid=(B, nH, nC),
        in_specs=[
            pl.BlockSpec(memory_space=pltpu.SMEM),
            pl.BlockSpec((C, HB * dk), lambda b, h, c: (b * nC + c, h)),
            pl.BlockSpec((C, HB * dk), lambda b, h, c: (b * nC + c, nH + h)),
            pl.BlockSpec((C, HB * dv), lambda b, h, c: (b * nC + c, voff + h)),
            pl.BlockSpec((C, HB * dv), lambda b, h, c: (b * nC + c, voff + nH + h)),
            pl.BlockSpec((C, dk // 2), lambda b, h, c: (c, 0)),
            pl.BlockSpec((C, dk // 2), lambda b, h, c: (c, 0)),
        ],
        out_specs=[
            pl.BlockSpec((C, HB * dv), lambda b, h, c: (b * nC + c, h)),
            pl.BlockSpec((1, HB, dk, dv), lambda b, h, c: (b, h, 0, 0)),
        ],
        out_shape=[
            jax.ShapeDtypeStruct((M, H * dv), F32),
            jax.ShapeDtypeStruct((B, H, dk, dv), F32),
        ],
        scratch_shapes=[pltpu.VMEM((HB, dk, dv), F32)],
        compiler_params=_params(("parallel", "parallel", "arbitrary")),
        name="retention_prompt",
    )(log_g, qkvg, qkvg, qkvg, qkvg, cos, sin)


def _ret_sample_kernel(lg_ref, q_ref, k_ref, v_ref, g_ref, cos_ref, sin_ref, s_in_ref, o_alias_ref,
                       o_ref, s_out_ref, *, T):
    del o_alias_ref
    hg = pl.program_id(1)
    R = q_ref.shape[0]
    HB, dk, dv = s_in_ref.shape[1:]
    cos, sin = cos_ref[...], sin_ref[...]
    ii = lax.broadcasted_iota(jnp.int32, (R, R), 0)
    jj = lax.broadcasted_iota(jnp.int32, (R, R), 1)
    rel = (ii - jj).astype(F32)
    same = (ii // T) == (jj // T)
    row = lax.broadcasted_iota(jnp.int32, (R, 1), 0)
    tok = (row % T).astype(F32)
    pad = RET_CHUNK - R
    for hh in range(HB):
        lg = lg_ref[hg * HB + hh]
        q = _rope_half(q_ref[:, hh * dk:(hh + 1) * dk], cos, sin)
        k = _rope_half(k_ref[:, hh * dk:(hh + 1) * dk], cos, sin) * (dk ** -0.5)
        v = v_ref[:, hh * dv:(hh + 1) * dv]
        qb = q.astype(BF16)
        decay = jnp.where(same, jnp.where(rel >= 0, jnp.exp(lg * jnp.maximum(rel, 0.0)), 0.0), 0.0)
        scores = _dot_nt(qb, k.astype(BF16)) * decay
        sc = scores.astype(BF16).astype(F32)
        vr = v.astype(BF16).astype(F32)
        inner = jnp.zeros((R, dv), F32)
        for j in range(R):
            inner = inner + sc[:, j:j + 1] * vr[j:j + 1, :]
        k_dec = k * jnp.exp((T - 1.0 - tok) * lg)
        g_all = jnp.exp(jnp.zeros((1, 1), F32) + T * lg)
        v_pad = jnp.concatenate([v, jnp.zeros((pad, dv), F32)], axis=0).astype(BF16)
        cross = jnp.zeros((R, dv), F32)
        for s in range(R // T):
            mine = (row // T) == s
            s_old = s_in_ref[s, hh]
            cross = jnp.where(mine, _dot(qb, s_old.astype(BF16)), cross)
            kd = jnp.where(mine, k_dec, 0.0)
            kd_t = jnp.concatenate([kd, jnp.zeros((pad, dk), F32)], axis=0).T.astype(BF16)
            s_out_ref[s, hh] = g_all * s_old + _dot(kd_t, v_pad)
        o = inner + cross * jnp.exp((tok + 1.0) * lg)
        o = o * _rms_scale(o, dv)
        gate = g_ref[:, hh * dv:(hh + 1) * dv]
        o_ref[:, hh * dv:(hh + 1) * dv] = (gate * jax.nn.sigmoid(gate)) * o


def _retention_sample(qkvg, log_g, cos, sin, state, gated, *, Bd, T, H, dk, dv, Mp):
    R = 8
    assert R % T == 0 and Bd % (R // T) == 0 and Mp % R == 0
    spb = R // T
    rb0 = Mp // R
    HB = math.gcd(H, 4)
    assert (2 * H * dk) % (HB * dv) == 0
    nH = H // HB
    voff = 2 * H * dk // (HB * dv)
    M = qkvg.shape[0]
    return pl.pallas_call(
        functools.partial(_ret_sample_kernel, T=T),
        grid=(Bd // spb, nH),
        in_specs=[
            pl.BlockSpec(memory_space=pltpu.SMEM),
            pl.BlockSpec((R, HB * dk), lambda i, h: (rb0 + i, h)),
            pl.BlockSpec((R, HB * dk), lambda i, h: (rb0 + i, nH + h)),
            pl.BlockSpec((R, HB * dv), lambda i, h: (rb0 + i, voff + h)),
            pl.BlockSpec((R, HB * dv), lambda i, h: (rb0 + i, voff + nH + h)),
            pl.BlockSpec((R, dk // 2), lambda i, h: (0, 0)),
            pl.BlockSpec((R, dk // 2), lambda i, h: (0, 0)),
            pl.BlockSpec((spb, HB, dk, dv), lambda i, h: (i, h, 0, 0)),
            pl.BlockSpec(memory_space=pl.ANY),
        ],
        out_specs=[
            pl.BlockSpec((R, HB * dv), lambda i, h: (rb0 + i, h)),
            pl.BlockSpec((spb, HB, dk, dv), lambda i, h: (i, h, 0, 0)),
        ],
        out_shape=[
            jax.ShapeDtypeStruct((M, H * dv), F32),
            jax.ShapeDtypeStruct((Bd, H, dk, dv), F32),
        ],
        input_output_aliases={8: 0},
        compiler_params=_params(("parallel", "parallel")),
        name="retention_sample",
    )(log_g, qkvg, qkvg, qkvg, qkvg, cos, sin, state, gated)


def _router_kernel(y_ref, g_ref, wr_ref, br_ref, route_ref, cnt_ref, carry_ref):
    i = pl.program_id(0)
    tm = y_ref.shape[0]
    NE = N_GROUPS * EXP_PER_GROUP

    @pl.when(i == 0)
    def _():
        carry_ref[...] = jnp.zeros_like(carry_ref)

    y = y_ref[...]
    hn = y * _rms_scale(y, y.shape[-1]) * g_ref[...]
    lt = lax.dot_general(wr_ref[...], hn, (((1,), (1,)), ((), ())),
                         precision=lax.Precision.HIGHEST, preferred_element_type=F32) + br_ref[...]
    gl = [lt[k:k + 1, :] for k in range(N_GROUPS)]
    gmax = functools.reduce(jnp.maximum, gl)
    grp = jnp.full(gmax.shape, N_GROUPS - 1, jnp.int32)
    for k in range(N_GROUPS - 2, -1, -1):
        grp = jnp.where(gl[k] == gmax, k, grp)
    pg = 1.0 / functools.reduce(lambda a, b: a + b, [jnp.exp(x - gmax) for x in gl])
    el = []
    for k in range(EXP_PER_GROUP):
        sel = lt[N_GROUPS + (N_GROUPS - 1) * EXP_PER_GROUP + k:N_GROUPS + (N_GROUPS - 1) * EXP_PER_GROUP + k + 1, :]
        for gi in range(N_GROUPS - 2, -1, -1):
            r = N_GROUPS + gi * EXP_PER_GROUP + k
            sel = jnp.where(grp == gi, lt[r:r + 1, :], sel)
        el.append(sel)
    v1 = functools.reduce(jnp.maximum, el)
    i1 = jnp.full(v1.shape, EXP_PER_GROUP - 1, jnp.int32)
    for k in range(EXP_PER_GROUP - 2, -1, -1):
        i1 = jnp.where(el[k] == v1, k, i1)
    el2 = [jnp.where(i1 == k, -jnp.inf, el[k]) for k in range(EXP_PER_GROUP)]
    v2 = functools.reduce(jnp.maximum, el2)
    i2 = jnp.full(v2.shape, EXP_PER_GROUP - 1, jnp.int32)
    for k in range(EXP_PER_GROUP - 2, -1, -1):
        i2 = jnp.where(el2[k] == v2, k, i2)
    e21 = jnp.exp(v2 - v1)
    den = 1.0 / (1.0 + e21)
    c1 = pg * den
    c2 = pg * (e21 * den)
    e1 = grp * EXP_PER_GROUP + i1
    e2 = grp * EXP_PER_GROUP + i2
    eid = lax.broadcasted_iota(jnp.int32, (NE, tm), 0)
    oh1 = (eid == e1).astype(F32)
    oh2 = (eid == e2).astype(F32)
    oh = oh1 + oh2
    tr = lax.broadcasted_iota(jnp.int32, (tm, tm), 0)
    tc = lax.broadcasted_iota(jnp.int32, (tm, tm), 1)
    before = jnp.where(tr < tc, 1.0, 0.0).astype(BF16)
    base = carry_ref[...] + _dot(oh.astype(BF16), before)
    r1 = jnp.sum(oh1 * base, axis=0, keepdims=True)
    r2 = jnp.sum(oh2 * base, axis=0, keepdims=True)
    total = carry_ref[...] + jnp.sum(oh, axis=1, keepdims=True)
    carry_ref[...] = total
    cnt_ref[...] = jnp.broadcast_to(total, cnt_ref.shape)
    route_ref[...] = jnp.concatenate(
        [e1.astype(F32), e2.astype(F32), c1, c2, r1, r2, jnp.zeros((2, tm), F32)], axis=0)


def _router(y, g, w_gr, b_gr, w_er, b_er, *, tm):
    M, D = y.shape
    NE = N_GROUPS * EXP_PER_GROUP
    rows = 32
    wr = jnp.zeros((rows, D), F32).at[:N_GROUPS].set(w_gr.T).at[N_GROUPS:N_GROUPS + NE].set(w_er.T)
    br = jnp.zeros((rows, 1), F32).at[:N_GROUPS, 0].set(b_gr).at[N_GROUPS:N_GROUPS + NE, 0].set(b_er)
    return pl.pallas_call(
        _router_kernel,
        grid=(M // tm,),
        in_specs=[
            pl.BlockSpec((tm, D), lambda i: (i, 0)),
            pl.BlockSpec((1, D), lambda i: (0, 0)),
            pl.BlockSpec((rows, D), lambda i: (0, 0)),
            pl.BlockSpec((rows, 1), lambda i: (0, 0)),
        ],
        out_specs=[
            pl.BlockSpec((8, tm), lambda i: (0, i)),
            pl.BlockSpec((NE, 128), lambda i: (0, 0)),
        ],
        out_shape=[
            jax.ShapeDtypeStruct((8, M), F32),
            jax.ShapeDtypeStruct((NE, 128), F32),
        ],
        scratch_shapes=[pltpu.VMEM((NE, 1), F32)],
        compiler_params=_params(("arbitrary",)),
        name="moe_router",
    )(y, g.reshape(1, D), wr, br)


def _ffn_kernel(te_ref, nu_ref, x_ref, g_ref, w1_ref, w3_ref, w2_ref, o_ref, xb_ref):
    del te_ref
    i = pl.program_id(0)
    f = pl.program_id(1)
    used = i < nu_ref[0]

    @pl.when(jnp.logical_and(used, f == 0))
    def _():
        x = x_ref[...]
        xb_ref[...] = (x * _rms_scale(x, x.shape[-1]) * g_ref[...]).astype(BF16)

    @pl.when(used)
    def _():
        x = xb_ref[...]
        a = _dot(x, w1_ref[0, 0].astype(BF16))
        b = _dot(x, w3_ref[0, 0].astype(BF16))
        hh = (a * jax.nn.sigmoid(a)) * b
        y = _dot(hh.astype(BF16), w2_ref[0, 0].astype(BF16))

        @pl.when(f == 0)
        def _():
            o_ref[...] = y

        @pl.when(f > 0)
        def _():
            o_ref[...] += y

    @pl.when(jnp.logical_and(jnp.logical_not(used), f == 0))
    def _():
        o_ref[...] = jnp.zeros_like(o_ref)


def _expert_ffn(xs, g, tile_expert, n_used, w1, w3, w2, *, layer, tm, tf):
    NS, D = xs.shape
    FF = w1.shape[3]
    nT = NS // tm

    def xi(i, f, te, nu):
        return (jnp.minimum(i, nu[0] - 1), 0)

    return pl.pallas_call(
        _ffn_kernel,
        grid_spec=pltpu.PrefetchScalarGridSpec(
            num_scalar_prefetch=2,
            grid=(nT, FF // tf),
            in_specs=[
                pl.BlockSpec((tm, D), xi),
                pl.BlockSpec((1, D), lambda i, f, te, nu: (0, 0)),
                pl.BlockSpec((1, 1, D, tf), lambda i, f, te, nu: (layer, te[i], 0, f)),
                pl.BlockSpec((1, 1, D, tf), lambda i, f, te, nu: (layer, te[i], 0, f)),
                pl.BlockSpec((1, 1, tf, D), lambda i, f, te, nu: (layer, te[i], f, 0)),
            ],
            out_specs=pl.BlockSpec((tm, D), lambda i, f, te, nu: (i, 0)),
            scratch_shapes=[pltpu.VMEM((tm, D), BF16)],
        ),
        out_shape=jax.ShapeDtypeStruct((NS, D), F32),
        compiler_params=_params(("arbitrary", "arbitrary")),
        name="moe_ffn",
    )(tile_expert, n_used, xs, g.reshape(1, D), w1, w3, w2)


def _combine_kernel(y_ref, a_ref, b_ref, c_ref, o_ref):
    c = c_ref[...]
    o_ref[...] = y_ref[...] + (c[:, 0:1] * a_ref[...] + c[:, 1:2] * b_ref[...])


def _combine(y, a, b, c, *, tm, row0=0, rows=None):
    M, D = y.shape
    rows = M - row0 if rows is None else rows
    assert row0 % tm == 0 and rows % tm == 0
    rb0 = row0 // tm
    row = pl.BlockSpec((tm, D), lambda i: (rb0 + i, 0))
    return pl.pallas_call(
        _combine_kernel,
        grid=(rows // tm,),
        in_specs=[row, row, row, pl.BlockSpec((tm, 2), lambda i: (rb0 + i, 0))],
        out_specs=pl.BlockSpec((tm, D), lambda i: (i, 0)),
        out_shape=jax.ShapeDtypeStruct((rows, D), F32),
        compiler_params=_params(("parallel",)),
        name="moe_combine",
    )(y, a, b, c)


def _moe(y, g, w_gr, b_gr, w_er, b_er, w1, w3, w2, *, layer, tm, split=None):
    M, D = y.shape
    NE = N_GROUPS * EXP_PER_GROUP
    route, cnt = _router(y, g, w_gr, b_gr, w_er, b_er, tm=tm)
    e1 = route[0].astype(jnp.int32)
    e2 = route[1].astype(jnp.int32)
    r1 = route[4].astype(jnp.int32)
    r2 = route[5].astype(jnp.int32)
    counts = cnt[:, 0].astype(jnp.int32)
    padded = ((counts + tm - 1) // tm) * tm
    ends = jnp.cumsum(padded)
    off = ends - padded
    d1 = off[e1] + r1
    d2 = off[e2] + r2
    NS = ((2 * M + NE * (tm - 1)) // tm) * tm
    tok = jnp.arange(M, dtype=jnp.int32)
    src = (jnp.arange(NS, dtype=jnp.int32) % M).at[jnp.concatenate([d1, d2])].set(
        jnp.concatenate([tok, tok]), unique_indices=True, mode="promise_in_bounds")
    n_used = (ends[-1] // tm).astype(jnp.int32).reshape(1)
    tile_start = jnp.minimum(jnp.arange(NS // tm, dtype=jnp.int32) * tm, ends[-1] - tm)
    tile_expert = jnp.sum((ends[None, :] <= tile_start[:, None]).astype(jnp.int32), axis=1)
    tile_expert = jnp.minimum(tile_expert, NE - 1)
    xs = y.at[src].get(mode="promise_in_bounds")
    tf = _tile(w1.shape[3], 256)
    ys = _expert_ffn(xs, g, tile_expert, n_used, w1, w3, w2, layer=layer, tm=tm, tf=tf)
    a = ys.at[d1].get(mode="promise_in_bounds")
    b = ys.at[d2].get(mode="promise_in_bounds")
    c = route[2:4].T
    if split is None:
        return _combine(y, a, b, c, tm=tm)
    return _combine(y, a, b, c, tm=tm, rows=split), _combine(y, a, b, c, tm=tm, row0=split)


def _latent_kernel(x_ref, gs_ref, w_ref, gc_ref, ckv_ref, kpe_ref, kpew_ref):
    x = x_ref[...]
    s = x * _rms_scale(x, x.shape[-1]) * gs_ref[...]
    ck = _dot(s.astype(BF16), w_ref[...])
    lora = ckv_ref.shape[1]
    wide = kpew_ref.shape[1]
    c = ck[:, :lora]
    ckv_ref[...] = c * _rms_scale(c, lora) * gc_ref[...]
    kpew_ref[...] = ck[:, lora:lora + wide]
    kpe_ref[...] = ck[:, lora + wide:]


def _latent(y, g_src, w_dkv, g_ckv, *, tm):
    M, D = y.shape
    lora = g_ckv.shape[0]
    w_pe = w_dkv[:, lora:]
    w_all = jnp.concatenate([w_dkv[:, :lora], _spread(w_pe), w_pe], axis=1).astype(BF16)
    N = w_all.shape[1]
    return pl.pallas_call(
        _latent_kernel,
        grid=(M // tm,),
        in_specs=[
            pl.BlockSpec((tm, D), lambda i: (i, 0)),
            pl.BlockSpec((1, D), lambda i: (0, 0)),
            pl.BlockSpec((D, N), lambda i: (0, 0)),
            pl.BlockSpec((1, lora), lambda i: (0, 0)),
        ],
        out_specs=[
            pl.BlockSpec((tm, lora), lambda i: (i, 0)),
            pl.BlockSpec((tm, ROPE_DIM), lambda i: (i, 0)),
            pl.BlockSpec((tm, 2 * ROPE_DIM), lambda i: (i, 0)),
        ],
        out_shape=[
            jax.ShapeDtypeStruct((M, lora), F32),
            jax.ShapeDtypeStruct((M, ROPE_DIM), F32),
            jax.ShapeDtypeStruct((M, 2 * ROPE_DIM), F32),
        ],
        compiler_params=_params(("parallel",)),
        name="shared_latent",
    )(y, g_src.reshape(1, D), w_all, g_ckv.reshape(1, lora))


def _spread(x):
    half = ROPE_DIM // 2
    z = jnp.zeros(x.shape[:-1] + (half,), x.dtype)
    return jnp.concatenate([x[..., :half], z, x[..., half:], z], axis=-1)


def _head_norm_rope(nope, pe, gain, cw, sw):
    ssq = jnp.sum(nope * nope + pe * pe, axis=-1, keepdims=True)
    inv = lax.rsqrt(ssq * (1.0 / QK_DIM) + EPS)
    n = nope * inv * gain[:, :NOPE_DIM]
    p = pe * inv * gain[:, NOPE_DIM:]
    p = p * cw + pltpu.roll(p, ROPE_DIM, axis=1) * sw
    return jnp.concatenate([n, p], axis=-1)


def _kv_prep_kernel(ckv_ref, kpe_ref, wuk_ref, wuv_ref, gk_ref, cos_ref, sin_ref, k_ref, v_ref):
    cb = ckv_ref[...].astype(BF16)
    kn = _dot(cb, wuk_ref[...])
    v_ref[...] = _dot(cb, wuv_ref[...]).astype(v_ref.dtype)
    pe, gain, cw, sw = kpe_ref[...], gk_ref[...], cos_ref[...], sin_ref[...]
    for hh in range(k_ref.shape[0]):
        k_ref[hh] = _head_norm_rope(kn[:, hh * NOPE_DIM:(hh + 1) * NOPE_DIM], pe, gain, cw, sw).astype(k_ref.dtype)


def _kv_prep(ckv, kpe_wide, w_uk, w_uv, g_wide, cw, sw, *, rows, H, tm):
    lora = ckv.shape[1]
    HB = math.gcd(H, 4)
    return pl.pallas_call(
        _kv_prep_kernel,
        grid=(rows // tm, H // HB),
        in_specs=[
            pl.BlockSpec((tm, lora), lambda i, h: (i, 0)),
            pl.BlockSpec((tm, 2 * ROPE_DIM), lambda i, h: (i, 0)),
            pl.BlockSpec((lora, HB * NOPE_DIM), lambda i, h: (0, h)),
            pl.BlockSpec((lora, HB * V_DIM), lambda i, h: (0, h)),
            pl.BlockSpec((1, QK_PAD), lambda i, h: (0, 0)),
            pl.BlockSpec((tm, 2 * ROPE_DIM), lambda i, h: (i, 0)),
            pl.BlockSpec((tm, 2 * ROPE_DIM), lambda i, h: (i, 0)),
        ],
        out_specs=[
            pl.BlockSpec((HB, tm, QK_PAD), lambda i, h: (h, i, 0)),
            pl.BlockSpec((tm, HB * V_DIM), lambda i, h: (i, h)),
        ],
        out_shape=[
            jax.ShapeDtypeStruct((H, rows, QK_PAD), BF16),
            jax.ShapeDtypeStruct((rows, H * V_DIM), BF16),
        ],
        compiler_params=_params(("parallel", "arbitrary")),
        name="mla_kv_prep",
    )(ckv, kpe_wide, w_uk, w_uv, g_wide.reshape(1, QK_PAD), cw, sw)


def _q_prep_kernel(q_ref, gq_ref, cos_ref, sin_ref, o_ref):
    gain, cw, sw = gq_ref[...], cos_ref[...], sin_ref[...]
    for hh in range(o_ref.shape[0]):
        q = q_ref[:, hh * QK_PAD:(hh + 1) * QK_PAD]
        o_ref[hh] = _head_norm_rope(q[:, :NOPE_DIM], q[:, NOPE_DIM:], gain, cw, sw).astype(o_ref.dtype)


def _q_prep(qraw, g_wide, cw, sw, *, H, tm):
    M = qraw.shape[0]
    HB = math.gcd(H, 4)
    return pl.pallas_call(
        _q_prep_kernel,
        grid=(M // tm, H // HB),
        in_specs=[
            pl.BlockSpec((tm, HB * QK_PAD), lambda i, h: (i, h)),
            pl.BlockSpec((1, QK_PAD), lambda i, h: (0, 0)),
            pl.BlockSpec((tm, 2 * ROPE_DIM), lambda i, h: (i, 0)),
            pl.BlockSpec((tm, 2 * ROPE_DIM), lambda i, h: (i, 0)),
        ],
        out_specs=pl.BlockSpec((HB, tm, QK_PAD), lambda i, h: (h, i, 0)),
        out_shape=jax.ShapeDtypeStruct((H, M, QK_PAD), BF16),
        compiler_params=_params(("parallel", "arbitrary")),
        name="mla_q_prep",
    )(qraw, g_wide.reshape(1, QK_PAD), cw, sw)


def _flash_kernel(q_ref, k_ref, v_ref, o_ref, m_ref, acc_ref):
    qi = pl.program_id(2)
    HB, tq, _ = q_ref.shape
    scale = QK_DIM ** -0.5
    m_ref[...] = jnp.full_like(m_ref, -jnp.inf)
    acc_ref[...] = jnp.zeros_like(acc_ref)
    ones = jnp.ones((tq, V_DIM), BF16)

    def step(start, masked):
        for h in range(HB):
            k = k_ref[h, pl.ds(start, tq), :]
            v = jnp.concatenate([v_ref[pl.ds(start, tq), h * V_DIM:(h + 1) * V_DIM], ones], axis=1)
            s = _dot_nt(q_ref[h], k) * scale
            if masked:
                row = lax.broadcasted_iota(jnp.int32, (tq, tq), 0)
                col = lax.broadcasted_iota(jnp.int32, (tq, tq), 1)
                s = jnp.where(col <= row, s, -jnp.inf)
            m_old = m_ref[h]
            m_new = jnp.maximum(m_old, jnp.max(s, axis=-1, keepdims=True))
            p = jnp.exp(s - m_new)
            acc_ref[h] = jnp.exp(m_old - m_new) * acc_ref[h] + _dot(p.astype(BF16), v)
            m_ref[h] = m_new

    def full_tile(ki, carry):
        step(pl.multiple_of(ki * tq, tq), False)
        return carry

    lax.fori_loop(0, qi, full_tile, 0)
    step(pl.multiple_of(qi * tq, tq), True)
    for h in range(HB):
        acc = acc_ref[h]
        o_ref[:, h * V_DIM:(h + 1) * V_DIM] = acc[:, :V_DIM] / acc[:, V_DIM:V_DIM + 1]


def _flash_prompt(qpad, kpad, v, *, B, T, H, M, tq):
    nq = T // tq
    HB = math.gcd(H, 4)
    return pl.pallas_call(
        _flash_kernel,
        grid=(B, H // HB, nq),
        in_specs=[
            pl.BlockSpec((HB, tq, QK_PAD), lambda b, h, qi: (h, b * nq + qi, 0)),
            pl.BlockSpec((HB, T, QK_PAD), lambda b, h, qi: (h, b, 0)),
            pl.BlockSpec((T, HB * V_DIM), lambda b, h, qi: (b, h)),
        ],
        out_specs=pl.BlockSpec((tq, HB * V_DIM), lambda b, h, qi: (b * nq + qi, h)),
        out_shape=jax.ShapeDtypeStruct((M, H * V_DIM), F32),
        scratch_shapes=[pltpu.VMEM((HB, tq, 1), F32), pltpu.VMEM((HB, tq, 2 * V_DIM), F32)],
        compiler_params=_params(("parallel", "parallel", "arbitrary")),
        name="mla_prompt_attention",
    )(qpad, kpad, v)


def _absorb_kernel(q_ref, wuk_ref, gk_ref, o_ref):
    qg = q_ref[0][:, :NOPE_DIM].astype(F32) * gk_ref[...]
    o_ref[0] = _dot_nt(qg.astype(BF16), wuk_ref[...]).astype(o_ref.dtype)


def _absorb(qpad, w_uk, g_k, *, H, Mp, Ms):
    lora = w_uk.shape[0]
    assert Mp % Ms == 0
    return pl.pallas_call(
        _absorb_kernel,
        grid=(H,),
        in_specs=[
            pl.BlockSpec((1, Ms, QK_PAD), lambda h: (h, Mp // Ms, 0)),
            pl.BlockSpec((lora, NOPE_DIM), lambda h: (0, h)),
            pl.BlockSpec((1, NOPE_DIM), lambda h: (0, 0)),
        ],
        out_specs=pl.BlockSpec((1, Ms, lora), lambda h: (h, 0, 0)),
        out_shape=jax.ShapeDtypeStruct((H, Ms, lora), BF16),
        compiler_params=_params(("parallel",)),
        name="mla_absorb_queries",
    )(qpad, w_uk, g_k[:NOPE_DIM].reshape(1, NOPE_DIM))


def _decode_kernel(pt_ref, qabs_ref, qpe_ref, wt_ref, *refs, T, H, P):
    del pt_ref
    ckv_pages = refs[:P]
    kpe_pages = refs[P:2 * P]
    (cnew_ref, knew_ref, cos_ref, sin_ref, cosn_ref, sinn_ref, g1_ref, g2_ref,
     o_ref, m_ref, l_ref, acc_ref) = refs[2 * P:]
    b = pl.program_id(0)
    t = pl.program_id(1)
    last = pl.num_programs(1) - 1
    page = ckv_pages[0].shape[1]
    lora = ckv_pages[0].shape[2]
    R = T * H
    NR = cnew_ref.shape[0]
    pps = SUB_POS // page

    @pl.when(t == 0)
    def _():
        m_ref[...] = jnp.full_like(m_ref, -jnp.inf)
        l_ref[...] = jnp.zeros_like(l_ref)
        acc_ref[...] = jnp.zeros_like(acc_ref)

    def scores(cb, kpe_t, cos, sin):
        W = cb.shape[0]
        kn = _dot_nt(wt_ref[...], cb)
        ssq = jnp.sum((kn * kn).reshape(H, NOPE_DIM, W), axis=1)
        sn = _dot_nt(qabs_ref[0], cb)
        ssq = ssq + jnp.sum(kpe_t * kpe_t, axis=0, keepdims=True)
        inv = lax.rsqrt(ssq * (1.0 / QK_DIM) + EPS)
        x1 = kpe_t[:ROPE_DIM // 2] * g1_ref[...]
        x2 = kpe_t[ROPE_DIM // 2:] * g2_ref[...]
        kr = jnp.concatenate([x1 * cos - x2 * sin, x1 * sin + x2 * cos], axis=0)
        sr = _dot(qpe_ref[0], kr.astype(BF16))
        return (sn + sr) * jnp.concatenate([inv] * T, axis=0) * (QK_DIM ** -0.5)

    def softmax_update(s, cb):
        m_old = m_ref[...]
        m_new = jnp.maximum(m_old, jnp.max(s, axis=-1, keepdims=True))
        alpha = jnp.exp(m_old - m_new)
        pr = jnp.exp(s - m_new)
        l_ref[...] = alpha * l_ref[...] + jnp.sum(pr, axis=-1, keepdims=True)
        acc_ref[...] = alpha * acc_ref[...] + _dot(pr.astype(BF16), cb)
        m_ref[...] = m_new

    cbs, ss = [], []
    for j in range(P // pps):
        pages = range(j * pps, (j + 1) * pps)
        cb = jnp.concatenate([ckv_pages[p][0].astype(BF16) for p in pages], axis=0)
        kpe_t = jnp.concatenate([kpe_pages[p][0] for p in pages], axis=1)
        cbs.append(cb)
        ss.append(scores(cb, kpe_t, cos_ref[:, j * SUB_POS:(j + 1) * SUB_POS],
                         sin_ref[:, j * SUB_POS:(j + 1) * SUB_POS]))
    softmax_update(jnp.concatenate(ss, axis=1), jnp.concatenate(cbs, axis=0))

    @pl.when(t == last)
    def _():
        cb = jnp.concatenate([cnew_ref[...], jnp.zeros((page - NR, lora), F32)], axis=0).astype(BF16)
        q_tok = lax.broadcasted_iota(jnp.int32, (R, page), 0) // H
        lane = lax.broadcasted_iota(jnp.int32, (R, page), 1)
        valid = jnp.logical_and(lane // T == b % (NR // T), lane % T <= q_tok)
        valid = jnp.logical_and(valid, lane < NR)
        s = scores(cb, knew_ref[0], cosn_ref[...], sinn_ref[...])
        softmax_update(jnp.where(valid, s, -jnp.inf), cb)
        o_ref[0] = acc_ref[...] / l_ref[...]


def _decode_attention(qabs, qpe, w_uk_t, cache_ckv, cache_kpe_t, page_table, ckv_all, kpe_new_t,
                      cos_t, sin_t, g_k, *, Bd, T, H, Mp):
    n_pages = page_table.shape[1]
    P = math.gcd(n_pages, PAGES_PER_STEP)
    page = cache_ckv.shape[1]
    lora = cache_ckv.shape[2]
    assert n_pages % P == 0 and SUB_POS % page == 0 and (P * page) % SUB_POS == 0
    steps = n_pages // P
    R = T * H
    NR = 8
    assert NR % T == 0 and Mp % NR == 0
    spb = NR // T
    pt = page_table.reshape(-1).astype(jnp.int32)
    half = ROPE_DIM // 2
    g1 = g_k[NOPE_DIM:NOPE_DIM + half].reshape(half, 1)
    g2 = g_k[NOPE_DIM + half:].reshape(half, 1)

    def page_spec(p, shape):
        def idx(b, t, pt):
            return (pt[b * n_pages + t * P + p], 0, 0)
        return pl.BlockSpec((1,) + shape, idx)

    in_specs = [
        pl.BlockSpec((1, R, lora), lambda b, t, pt: (b, 0, 0)),
        pl.BlockSpec((1, R, ROPE_DIM), lambda b, t, pt: (b, 0, 0)),
        pl.BlockSpec(w_uk_t.shape, lambda b, t, pt: (0, 0)),
    ]
    in_specs += [page_spec(p, (page, lora)) for p in range(P)]
    in_specs += [page_spec(p, (ROPE_DIM, page)) for p in range(P)]
    in_specs += [
        pl.BlockSpec((NR, lora), lambda b, t, pt: (Mp // NR + b // spb, 0)),
        pl.BlockSpec((1, ROPE_DIM, page), lambda b, t, pt: (b // spb, 0, 0)),
        pl.BlockSpec((half, P * page), lambda b, t, pt: (0, t)),
        pl.BlockSpec((half, P * page), lambda b, t, pt: (0, t)),
        pl.BlockSpec((half, page), lambda b, t, pt: (0, n_pages)),
        pl.BlockSpec((half, page), lambda b, t, pt: (0, n_pages)),
        pl.BlockSpec((half, 1), lambda b, t, pt: (0, 0)),
        pl.BlockSpec((half, 1), lambda b, t, pt: (0, 0)),
    ]
    return pl.pallas_call(
        functools.partial(_decode_kernel, T=T, H=H, P=P),
        grid_spec=pltpu.PrefetchScalarGridSpec(
            num_scalar_prefetch=1,
            grid=(Bd, steps),
            in_specs=in_specs,
            out_specs=pl.BlockSpec((1, R, lora), lambda b, t, pt: (b, 0, 0)),
            scratch_shapes=[
                pltpu.VMEM((R, 1), F32),
                pltpu.VMEM((R, 1), F32),
                pltpu.VMEM((R, lora), F32),
            ],
        ),
        out_shape=jax.ShapeDtypeStruct((Bd, R, lora), F32),
        compiler_params=_params(("parallel", "arbitrary")),
        name="mla_decode_attention",
    )(pt, qabs, qpe, w_uk_t, *([cache_ckv] * P), *([cache_kpe_t] * P), ckv_all, kpe_new_t, cos_t, sin_t, cos_t, sin_t,
      g1, g2)


def _head_values_kernel(c_ref, w_ref, alias_ref, o_ref):
    del alias_ref
    o_ref[...] = _dot(c_ref[...].astype(BF16), w_ref[...])


def _head_values(ctx, w_uv, attn, *, H, Mp, Ms):
    lora = w_uv.shape[0]
    return pl.pallas_call(
        _head_values_kernel,
        grid=(H,),
        in_specs=[
            pl.BlockSpec((Ms, lora), lambda h: (0, h)),
            pl.BlockSpec((lora, V_DIM), lambda h: (0, h)),
            pl.BlockSpec(memory_space=pl.ANY),
        ],
        out_specs=pl.BlockSpec((Ms, V_DIM), lambda h: (Mp // Ms, h)),
        out_shape=jax.ShapeDtypeStruct(attn.shape, F32),
        input_output_aliases={2: 0},
        compiler_params=_params(("parallel",)),
        name="mla_decode_values",
    )(ctx, w_uv, attn)


def _rope_tables(pos, d):
    freq = ROPE_THETA ** (-jnp.arange(0, d, 2, dtype=F32) / d)
    ang = pos.astype(F32)[:, None] * freq[None, :]
    return jnp.cos(ang), jnp.sin(ang)


def kernel(x_prompt, x_sample, state_ret, cache_ckv, cache_kpe, page_table, ln_mix, ln_ffn, ret_w_in, ret_w_out,
           kv_src_norm, w_dkv, ckv_norm, w_ukv, k_norm, mla_w_q, q_norm, mla_w_o, moe_w_gr, moe_b_gr, moe_w_er,
           moe_b_er, moe_w1, moe_w3, moe_w2):
    B, T, D = x_prompt.shape
    Bd, Td, _ = x_sample.shape
    n_a = state_ret.shape[0]
    depth = ln_mix.shape[0]
    RH, dk, dv = state_ret.shape[2:]
    lora, MH = w_ukv.shape[0], w_ukv.shape[1]
    Mp, Ms = B * T, Bd * Td
    M = Mp + Ms
    past = page_table.shape[1] * cache_ckv.shape[1]
    tm = _tile(math.gcd(Mp, Ms), 512)
    tm_p = _tile(Mp, 1024)

    pos_p = jnp.arange(T)
    pos_s = past + jnp.arange(Td)
    log_g = jnp.log1p(-jnp.exp2(-5.0 - jnp.arange(RH, dtype=F32)))
    y = jnp.concatenate([x_prompt.reshape(Mp, D), x_sample.reshape(Ms, D)], axis=0)

    ret_p, ret_s = [], []
    ckv = kpe = kpad = vals = None
    for i in range(depth):
        if i == n_a:
            ckv, kpe, kpe_wide = _latent(y, kv_src_norm, w_dkv, ckv_norm, tm=tm)
            pos_all = jnp.concatenate([jnp.tile(pos_p, B), jnp.tile(pos_s, Bd)])
            cos_m, sin_m = _rope_tables(pos_all, ROPE_DIM)
            cw = _spread(jnp.concatenate([cos_m, cos_m], axis=1))
            sw = _spread(jnp.concatenate([-sin_m, sin_m], axis=1))
            w_uk = w_ukv[..., :NOPE_DIM].reshape(lora, MH * NOPE_DIM).astype(BF16)
            w_uv = w_ukv[..., NOPE_DIM:].reshape(lora, MH * V_DIM).astype(BF16)
            gk_wide = jnp.concatenate([k_norm[:NOPE_DIM], _spread(k_norm[NOPE_DIM:])])
            kpad, vals = _kv_prep(ckv, kpe_wide, w_uk, w_uv, gk_wide, cw, sw, rows=Mp, H=MH, tm=tm)
        if i < n_a:
            w_in = ret_w_in[i].astype(BF16)
            qkvg = _matmul_streams(y, w_in, g=ln_mix[i], Mp=Mp, tm_p=tm_p, tm_s=tm,
                                   tn=_tile(w_in.shape[1], 1024), name="ret_in_proj")
            cos_p, sin_p = _rope_tables(pos_p, dk)
            gated, s_p = _retention_prompt(qkvg, log_g, cos_p, sin_p, B=B, T=T, H=RH, dk=dk, dv=dv, M=M)
            cos_s, sin_s = _rope_tables(jnp.tile(pos_s, 8 // Td), dk)
            gated, s_s = _retention_sample(qkvg, log_g, cos_s, sin_s, state_ret[i], gated,
                                           Bd=Bd, T=Td, H=RH, dk=dk, dv=dv, Mp=Mp)
            ret_p.append(s_p)
            ret_s.append(s_s)
            y = _matmul(gated, ret_w_out[i].astype(BF16), res=y, tm=tm, tn=_tile(D, 512), name="ret_out_proj")
        else:
            j = i - n_a
            wq = mla_w_q[j].reshape(D, MH, QK_DIM)
            wq = jnp.concatenate([wq[..., :NOPE_DIM], _spread(wq[..., NOPE_DIM:])], axis=-1)
            wq = wq.reshape(D, MH * QK_PAD).astype(BF16)
            qraw = _matmul_streams(y, wq, g=ln_mix[i], Mp=Mp, tm_p=tm_p, tm_s=tm,
                                   tn=_tile(MH * QK_PAD, 1024), name="mla_q_proj")
            gq_wide = jnp.concatenate([q_norm[j][:NOPE_DIM], _spread(q_norm[j][NOPE_DIM:])])
            qpad = _q_prep(qraw, gq_wide, cw, sw, H=MH, tm=tm)
            attn = _flash_prompt(qpad, kpad, vals, B=B, T=T, H=MH, M=M, tq=_tile(T, 512))
            qabs = _absorb(qpad, w_uk, k_norm, H=MH, Mp=Mp, Ms=Ms)
            qabs = qabs.reshape(MH, Bd, Td, lora).transpose(1, 2, 0, 3).reshape(Bd, Td * MH, lora)
            half = ROPE_DIM // 2
            qpe = jnp.concatenate([qpad[:, Mp:, NOPE_DIM:NOPE_DIM + half],
                                   qpad[:, Mp:, NOPE_DIM + ROPE_DIM:NOPE_DIM + ROPE_DIM + half]], axis=-1)
            qpe = qpe.reshape(MH, Bd, Td, ROPE_DIM).transpose(1, 2, 0, 3).reshape(Bd, Td * MH, ROPE_DIM)
            page = cache_ckv.shape[1]
            col = jnp.arange(past + page)
            pos_cols = jnp.where(col < past, col, past + (col - past) % Td)
            cos_c, sin_c = _rope_tables(pos_cols, ROPE_DIM)
            w_uk_t = w_uk.T
            spb = 8 // Td
            kpe_new_t = kpe[Mp:].reshape(Bd // spb, spb * Td, ROPE_DIM).swapaxes(1, 2)
            kpe_new_t = jnp.pad(kpe_new_t, ((0, 0), (0, 0), (0, page - spb * Td)))
            ctx = _decode_attention(qabs, qpe, w_uk_t, cache_ckv, cache_kpe.swapaxes(1, 2), page_table, ckv,
                                    kpe_new_t, cos_c.T, sin_c.T, k_norm, Bd=Bd, T=Td, H=MH, Mp=Mp)
            attn = _head_values(ctx.reshape(Ms, MH * lora), w_uv, attn, H=MH, Mp=Mp, Ms=Ms)
            y = _matmul_streams(attn, mla_w_o[j].astype(BF16), res=y, Mp=Mp, tm_p=tm_p, tm_s=tm,
                                tn=_tile(D, 512), name="mla_out_proj")
        y = _moe(y, ln_ffn[i], moe_w_gr[i], moe_b_gr[i], moe_w_er[i], moe_b_er[i],
                 moe_w1, moe_w3, moe_w2, layer=i, tm=tm, split=Mp if i == depth - 1 else None)

    yp = y[0].reshape(B, T, D)
    ys = y[1].reshape(Bd, Td, D)
    return (yp, ys, jnp.stack(ret_p, axis=0), jnp.stack(ret_s, axis=0),
            ckv[:Mp].reshape(B, T, lora), kpe[:Mp].reshape(B, T, ROPE_DIM),
            ckv[Mp:].reshape(Bd, Td, lora), kpe[Mp:].reshape(Bd, Td, ROPE_DIM))
```
